```python
import jax, jax.numpy as jnp
from jax import lax
import numpy as np

D_MODEL = 4096
BATCH = 2
SEQ = 4096
DEPTH = 4

GRID_W = 64
CTX_LEN = 256
MIX_WIDTH = D_MODEL // 4
N_BRANCH = 3
IN_COLS = 7 * MIX_WIDTH
RET_DK = 128
RET_DV = 128
N_RET_HEADS = MIX_WIDTH // RET_DV
RET_CHUNK = 128
ROPE_BASE = 10000.0
SG_CHUNK = 128
SG_GROUPS = 8
POOL_WINDOWS = (2, 4, 8, 16)
N_POOL = len(POOL_WINDOWS)
POOL_GROUP = MIX_WIDTH // N_POOL
GATE_RANK = D_MODEL // 16
N_EXPERTS = 32
TOP_K = 4
D_EXPERT = 160
SWIGLU_LIMIT = 7.0
SWIGLU_ALPHA = 1.702
N_MOD = 6
EPS = 1e-6

kernel_name = 'hybrid_retention_sgmlp_pool_moe_dit'

F32 = jnp.float32


def _rmsnorm(x, w):
    xf = x.astype(F32)
    y = xf * lax.rsqrt(jnp.mean(xf * xf, axis=-1, keepdims=True) + EPS)
    return (y * w.astype(F32)).astype(x.dtype)


def _layernorm(x, w):
    xf = x.astype(F32)
    mu = jnp.mean(xf, axis=-1, keepdims=True)
    var = jnp.mean(jnp.square(xf - mu), axis=-1, keepdims=True)
    return ((xf - mu) * lax.rsqrt(var + EPS) * w.astype(F32)).astype(x.dtype)


def _modulation(cond, ada_w, ada_b):
    m = jax.nn.silu(cond) @ ada_w + ada_b
    return m.reshape(cond.shape[:-1] + (N_MOD, cond.shape[-1]))


def _modulate(h, shift, scale):
    return h * (1.0 + scale[:, None]) + shift[:, None]


def _heads(a):
    B, N, _ = a.shape
    return a.reshape(B, N, N_RET_HEADS, -1).transpose(0, 2, 1, 3)


def _rotate(x, ang):
    m = x.shape[-1] // 2
    cos = jnp.cos(ang).astype(x.dtype)
    sin = jnp.sin(ang).astype(x.dtype)
    x1, x2 = x[..., :m], x[..., m:]
    return jnp.concatenate([x1 * cos - x2 * sin, x2 * cos + x1 * sin], axis=-1)


def _rope_2d(x, ang_row, ang_col):
    half = x.shape[-1] // 2
    return jnp.concatenate([_rotate(x[..., :half], ang_row), _rotate(x[..., half:], ang_col)], axis=-1)


def _retention_chunked(q, k, v, log_gamma, s0):
    B, H, N, _ = q.shape
    dv = v.shape[-1]
    C = RET_CHUNK
    nch = N // C
    pos = jnp.arange(C, dtype=F32)
    diff = pos[:, None] - pos[None, :]
    decay_mask = jnp.where(diff >= 0.0, jnp.exp(log_gamma[:, None, None] * jnp.maximum(diff, 0.0)), 0.0)[None]
    q_decay = jnp.exp(log_gamma[:, None] * (pos + 1.0))[None, :, :, None]
    k_decay = jnp.exp(log_gamma[:, None] * (C - 1.0 - pos))[None, :, :, None]
    chunk_decay = jnp.exp(log_gamma * C)[None, :, None, None]

    def to_chunks(a):
        return a.astype(F32).reshape(B, H, nch, C, a.shape[-1]).transpose(2, 0, 1, 3, 4)

    def step(state, blk):
        qc, kc, vc = blk
        scores = jnp.einsum('bhid,bhjd->bhij', qc, kc) * decay_mask
        y = (jnp.einsum('bhij,bhjv->bhiv', scores, vc)
             + jnp.einsum('bhid,bhdv->bhiv', qc * q_decay, state))
        state = chunk_decay * state + jnp.einsum('bhjd,bhjv->bhdv', kc * k_decay, vc)
        return state, y

    state, ys = lax.scan(step, s0, (to_chunks(q), to_chunks(k), to_chunks(v)))
    return ys.transpose(1, 2, 0, 3, 4).reshape(B, H, N, dv), state


def _retention_bidir(q, k, v, log_gamma, s_fwd, s_bwd):
    y_f, st_f = _retention_chunked(q, k, v, log_gamma[0], s_fwd)
    y_b, st_b = _retention_chunked(q[:, :, ::-1], k[:, :, ::-1], v[:, :, ::-1], log_gamma[1], s_bwd)
    return y_f + y_b[:, :, ::-1], st_f, st_b


def _retention_state(k, v, log_gamma_dir, reverse):
    L = k.shape[2]
    m = jnp.arange(L, dtype=F32)
    expo = m if reverse else (L - 1.0 - m)
    w = jnp.exp(log_gamma_dir[:, None] * expo[None, :])
    return jnp.einsum('bhmd,hm,bhmv->bhdv', k.astype(F32), w, v.astype(F32))


def _head_norm(y, w):
    B, H, N, dv = y.shape
    mu = jnp.mean(y, axis=-1, keepdims=True)
    var = jnp.mean(jnp.square(y - mu), axis=-1, keepdims=True)
    yn = (y - mu) * lax.rsqrt(var + EPS)
    return yn.transpose(0, 2, 1, 3).reshape(B, N, H * dv) * w.astype(F32)


def _spatial_gating(u, v, sg_norm_w, sg_w, sg_b):
    B, N, W = v.shape
    gc = W // SG_GROUPS
    v = _layernorm(v, sg_norm_w)
    vb = v.reshape(B, N // SG_CHUNK, SG_CHUNK, SG_GROUPS, gc)
    mixed = jnp.einsum('gpq,bkqgc->bkpgc', sg_w, vb) + sg_b.T[None, None, :, :, None]
    return u * mixed.reshape(B, N, W)


def _multiscale_pool(p):
    B, N, W = p.shape
    gw = W // N_POOL
    pf = p.astype(F32)
    cs = jnp.concatenate([jnp.zeros((B, 1, W), F32), jnp.cumsum(pf, axis=1)], axis=1)
    t = jnp.arange(N)
    outs = []
    for i, w in enumerate(POOL_WINDOWS):
        lo = jnp.clip(t - w // 2, 0, N)
        hi = jnp.clip(t + w // 2, 0, N)
        csg = cs[..., i * gw:(i + 1) * gw]
        mean = (csg[:, hi] - csg[:, lo]) / (hi - lo).astype(F32)[None, :, None]
        outs.append(mean - pf[..., i * gw:(i + 1) * gw])
    return jnp.concatenate(outs, axis=-1).astype(p.dtype)


def _branches(n, z, y_ret, ret_gn_w, sg_norm_w, sg_w, sg_b, pool_w, pool_b, pool_scale,
              w_branch, gate_down, gate_up, gate_b, w_out):
    W = MIX_WIDTH
    B, N, _ = n.shape
    y_a = jax.nn.silu(z[..., 3 * W:4 * W]) * _head_norm(y_ret, ret_gn_w).astype(n.dtype)
    y_b = _spatial_gating(jax.nn.gelu(z[..., 4 * W:5 * W]), jax.nn.gelu(z[..., 5 * W:6 * W]),
                          sg_norm_w, sg_w, sg_b)
    pooled = _multiscale_pool(z[..., 6 * W:7 * W]).reshape(B, N, N_POOL, POOL_GROUP)
    y_c = (jnp.einsum('bngc,gcd->bngd', pooled, pool_w).reshape(B, N, W) + pool_b) * pool_scale
    r = n @ gate_down
    terms = [jax.nn.sigmoid(r @ gate_up[i] + gate_b[i]) * (y @ w_branch[i])
             for i, y in enumerate((y_a, y_b, y_c))]
    return (terms[0] + terms[1] + terms[2]) @ w_out


def _mixer(nc, nx, ang_row, ang_col, w_in, ret_decay, ret_gn_w, sg_norm_w, sg_w, sg_b,
           pool_w, pool_b, pool_scale, w_branch, gate_down, gate_up, gate_b, w_out, with_ctx_out):
    W = MIX_WIDTH
    log_gamma = jax.nn.log_sigmoid(ret_decay.astype(F32))
    k_scale = RET_DK ** -0.5
    zx = nx @ w_in
    qx = _rope_2d(_heads(zx[..., 0:W]), ang_row, ang_col)
    kx = _rope_2d(_heads(zx[..., W:2 * W]), ang_row, ang_col) * k_scale
    vx = _heads(zx[..., 2 * W:3 * W])
    if with_ctx_out:
        zc = nc @ w_in
        qc = _heads(zc[..., 0:W])
        kc = _heads(zc[..., W:2 * W]) * k_scale
        vc = _heads(zc[..., 2 * W:3 * W])
        zero = jnp.zeros(kc.shape[:2] + (RET_DK, RET_DV), F32)
        yc, s_f, s_b = _retention_bidir(qc, kc, vc, log_gamma, zero, zero)
    else:
        zkv = nc @ w_in[:, W:3 * W]
        kc = _heads(zkv[..., :W]) * k_scale
        vc = _heads(zkv[..., W:])
        s_f = _retention_state(kc, vc, log_gamma[0], False)
        s_b = _retention_state(kc, vc, log_gamma[1], True)
    yx, _, _ = _retention_bidir(qx, kx, vx, log_gamma, s_f, s_b)
    out_x = _branches(nx, zx, yx, ret_gn_w, sg_norm_w, sg_w, sg_b, pool_w, pool_b, pool_scale,
                      w_branch, gate_down, gate_up, gate_b, w_out)
    if with_ctx_out:
        out_c = _branches(nc, zc, yc, ret_gn_w, sg_norm_w, sg_w, sg_b, pool_w, pool_b, pool_scale,
                          w_branch, gate_down, gate_up, gate_b, w_out)
        return out_c, out_x
    return None, out_x


def _moe(h, router_w, router_b, w1, b1, w2, b2):
    B, N, D = h.shape
    t = h.reshape(B * N, D)
    logits = (t @ router_w + router_b).astype(F32)
    top_val, top_idx = lax.top_k(logits, TOP_K)
    top_w = jax.nn.softmax(top_val, axis=-1).astype(h.dtype)
    combine = jnp.sum(jax.nn.one_hot(top_idx, N_EXPERTS, dtype=h.dtype) * top_w[..., None], axis=1)
    gu = jnp.einsum('td,edf->tef', t, w1) + b1
    glu = jnp.minimum(gu[..., ::2], SWIGLU_LIMIT)
    lin = jnp.clip(gu[..., 1::2], -SWIGLU_LIMIT, SWIGLU_LIMIT)
    act = glu * jax.nn.sigmoid(SWIGLU_ALPHA * glu) * (lin + 1.0) * combine[..., None]
    out = jnp.einsum('tef,efd->td', act, w2) + combine @ b2
    return out.reshape(B, N, D)


def setup_inputs(seed: int = 0) -> dict:
    key = jax.random.key(seed)
    ks = jax.random.split(key, 32)
    D, W, H, E, F, R = D_MODEL, MIX_WIDTH, N_RET_HEADS, N_EXPERTS, D_EXPERT, GATE_RANK

    def nrm(k, shape, scale):
        return jax.random.normal(k, shape, F32) * scale

    gamma0 = 1.0 - 2.0 ** (-5.0 - jnp.arange(H, dtype=F32))
    return {
        'x': nrm(ks[0], (BATCH, SEQ, D), 1.0),
        'c': nrm(ks[1], (BATCH, D), 1.0),
        'ctx': nrm(ks[2], (BATCH, CTX_LEN, D), 1.0),
        'c_ctx': nrm(ks[3], (D,), 1.0),
        'ada_w': nrm(ks[4], (DEPTH, D, N_MOD * D), 0.5 * D ** -0.5),
        'ada_b': nrm(ks[5], (DEPTH, N_MOD * D), 0.02),
        'norm1_w': 1.0 + nrm(ks[6], (DEPTH, D), 0.02),
        'norm2_w': 1.0 + nrm(ks[7], (DEPTH, D), 0.02),
        'w_in': nrm(ks[8], (DEPTH, D, IN_COLS), D ** -0.5),
        'ret_decay': jnp.log(gamma0 / (1.0 - gamma0))[None, None, :] + nrm(ks[9], (DEPTH, 2, H), 0.1),
        'ret_gn_w': 1.0 + nrm(ks[10], (DEPTH, W), 0.02),
        'sg_norm_w': 1.0 + nrm(ks[11], (DEPTH, W), 0.02),
        'sg_w': nrm(ks[12], (DEPTH, SG_GROUPS, SG_CHUNK, SG_CHUNK), 0.5 * SG_CHUNK ** -0.5),
        'sg_b': 1.0 + nrm(ks[13], (DEPTH, SG_GROUPS, SG_CHUNK), 0.02),
        'pool_w': nrm(ks[14], (DEPTH, N_POOL, POOL_GROUP, POOL_GROUP), POOL_GROUP ** -0.5),
        'pool_b': nrm(ks[15], (DEPTH, W), 0.02),
        'pool_scale': 1.0 + nrm(ks[16], (DEPTH, W), 0.02),
        'w_branch': nrm(ks[17], (DEPTH, N_BRANCH, W, D), W ** -0.5),
        'gate_down': nrm(ks[18], (DEPTH, D, R), D ** -0.5),
        'gate_up': nrm(ks[19], (DEPTH, N_BRANCH, R, D), R ** -0.5),
        'gate_b': nrm(ks[20], (DEPTH, N_BRANCH, D), 0.02),
        'w_out': nrm(ks[21], (DEPTH, D, D), D ** -0.5),
        'router_w': nrm(ks[22], (DEPTH, D, E), D ** -0.5),
        'router_b': nrm(ks[23], (DEPTH, E), 0.01),
        'w1': nrm(ks[24], (DEPTH, E, D, 2 * F), D ** -0.5),
        'b1': nrm(ks[25], (DEPTH, E, 2 * F), 0.02),
        'w2': nrm(ks[26], (DEPTH, E, F, D), F ** -0.5),
        'b2': nrm(ks[27], (DEPTH, E, D), 0.02),
        'final_norm_w': 1.0 + nrm(ks[28], (D,), 0.02),
    }


def reference(x, c, ctx, c_ctx, ada_w, ada_b, norm1_w, norm2_w, w_in, ret_decay, ret_gn_w,
              sg_norm_w, sg_w, sg_b, pool_w, pool_b, pool_scale, w_branch, gate_down, gate_up,
              gate_b, w_out, router_w, router_b, w1, b1, w2, b2, final_norm_w):
    n_lat = x.shape[1]
    n_ctx = ctx.shape[1]
    rows = n_lat // GRID_W
    row_pos = jnp.repeat(jnp.arange(rows, dtype=F32), GRID_W)
    col_pos = jnp.tile(jnp.arange(GRID_W, dtype=F32), rows)
    n_freq = RET_DK // 4
    freqs = ROPE_BASE ** (-jnp.arange(n_freq, dtype=F32) / n_freq)
    ang_row = row_pos[:, None] * freqs
    ang_col = col_pos[:, None] * freqs

    hx, hc = x, ctx
    for l in range(DEPTH):
        last = l == DEPTH - 1
        mod_x = _modulation(c, ada_w[l], ada_b[l])
        mod_c = _modulation(c_ctx, ada_w[l], ada_b[l])[None]
        nx = _modulate(_rmsnorm(hx, norm1_w[l]), mod_x[:, 0], mod_x[:, 1])
        nc = _modulate(_rmsnorm(hc, norm1_w[l]), mod_c[:, 0], mod_c[:, 1])
        out_c, out_x = _mixer(nc, nx, ang_row, ang_col, w_in[l], ret_decay[l], ret_gn_w[l],
                              sg_norm_w[l], sg_w[l], sg_b[l], pool_w[l], pool_b[l], pool_scale[l],
                              w_branch[l], gate_down[l], gate_up[l], gate_b[l], w_out[l], not last)
        hx = hx + mod_x[:, None, 2] * out_x
        nx2 = _modulate(_rmsnorm(hx, norm2_w[l]), mod_x[:, 3], mod_x[:, 4])
        if last:
            hx = hx + mod_x[:, None, 5] * _moe(nx2, router_w[l], router_b[l], w1[l], b1[l], w2[l], b2[l])
        else:
            hc = hc + mod_c[:, None, 2] * out_c
            nc2 = _modulate(_rmsnorm(hc, norm2_w[l]), mod_c[:, 3], mod_c[:, 4])
            f = _moe(jnp.concatenate([nc2, nx2], axis=1), router_w[l], router_b[l], w1[l], b1[l], w2[l], b2[l])
            hc = hc + mod_c[:, None, 5] * f[:, :n_ctx]
            hx = hx + mod_x[:, None, 5] * f[:, n_ctx:]
    return _rmsnorm(hx, final_norm_w)
```

```python
import functools
import math

import jax
import jax.numpy as jnp
from jax import lax
from jax.experimental import pallas as pl
from jax.experimental.pallas import tpu as pltpu

F32 = jnp.float32
BF16 = jnp.bfloat16

GRID_W = 64
N_RET_HEADS = 8
RET_CHUNK = 128
ROPE_BASE = 10000.0
SG_CHUNK = 128
SG_GROUPS = 8
POOL_WINDOWS = (2, 4, 8, 16)
N_EXPERTS = 32
TOP_K = 4
SWIGLU_LIMIT = 7.0
SWIGLU_ALPHA = 1.702
N_MOD = 6
EPS = 1e-6

LANES = 128
COND_ROWS = 8
NEG_BIG = -1e30
MIB = 1024 * 1024


def _cparams(sem, vmem_mib):
    return pltpu.CompilerParams(dimension_semantics=sem, vmem_limit_bytes=int(vmem_mib * MIB))


def _pick(n, cands):
    for c in cands:
        if n % c == 0:
            return c
    raise ValueError(f"no tile in {cands} divides {n}")


def _dot(a, b):
    return jnp.dot(a, b, preferred_element_type=F32)


def _sigmoid(x):
    return 1.0 / (1.0 + jnp.exp(-x))


def _gelu_tanh(x):
    return 0.5 * x * (1.0 + jnp.tanh(math.sqrt(2.0 / math.pi) * (x + 0.044715 * (x * x * x))))


def _adaln_body(cond_ref, w_ref, b_ref, o_ref):
    s = cond_ref[...]
    s = (s * _sigmoid(s)).astype(BF16)
    o_ref[0] = _dot(s, w_ref[0].astype(BF16)) + b_ref[0]


def _adaln(cond, ada_w, ada_b):
    L, D, M = ada_w.shape
    tn = _pick(M, (512, 256, 128))
    return pl.pallas_call(
        _adaln_body,
        out_shape=jax.ShapeDtypeStruct((L, COND_ROWS, M), F32),
        grid=(L, M // tn),
        in_specs=[
            pl.BlockSpec((COND_ROWS, D), lambda l, j: (0, 0)),
            pl.BlockSpec((1, D, tn), lambda l, j: (l, 0, j)),
            pl.BlockSpec((1, 1, tn), lambda l, j: (l, 0, j)),
        ],
        out_specs=pl.BlockSpec((1, COND_ROWS, tn), lambda l, j: (l, 0, j)),
        compiler_params=_cparams(("arbitrary", "arbitrary"), 40),
        name="adaln",
    )(cond, ada_w, ada_b.reshape(L, 1, M))


def _norm_mod_body(h_ref, w_ref, mod_ref, o_ref, *, shift_idx, scale_idx):
    x = h_ref[...]
    ms = jnp.mean(x * x, axis=-1, keepdims=True)
    y = x * lax.rsqrt(ms + EPS) * w_ref[...]
    shift = mod_ref[0, pl.ds(shift_idx, 1), :]
    scale = mod_ref[0, pl.ds(scale_idx, 1), :]
    o_ref[...] = (y * (1.0 + scale) + shift).astype(o_ref.dtype)


def _norm_mod(h, w, mod, group_of_tile, tm, shift_idx, scale_idx):
    T, D = h.shape
    return pl.pallas_call(
        functools.partial(_norm_mod_body, shift_idx=shift_idx, scale_idx=scale_idx),
        out_shape=jax.ShapeDtypeStruct((T, D), BF16),
        grid=(T // tm,),
        in_specs=[
            pl.BlockSpec((tm, D), lambda i: (i, 0)),
            pl.BlockSpec((1, D), lambda i: (0, 0)),
            pl.BlockSpec((1, N_MOD, D), lambda i: (group_of_tile(i), 0, 0)),
        ],
        out_specs=pl.BlockSpec((tm, D), lambda i: (i, 0)),
        compiler_params=_cparams(("arbitrary",), 40),
        name="norm_mod",
    )(h, w.reshape(1, D), mod)


def _final_norm_body(h_ref, w_ref, o_ref):
    x = h_ref[...]
    ms = jnp.mean(x * x, axis=-1, keepdims=True)
    o_ref[...] = x * lax.rsqrt(ms + EPS) * w_ref[...]


def _final_norm(h, w, rows, tm):
    D = h.shape[1]
    return pl.pallas_call(
        _final_norm_body,
        out_shape=jax.ShapeDtypeStruct((rows, D), F32),
        grid=(rows // tm,),
        in_specs=[pl.BlockSpec((tm, D), lambda i: (i, 0)), pl.BlockSpec((1, D), lambda i: (0, 0))],
        out_specs=pl.BlockSpec((tm, D), lambda i: (i, 0)),
        compiler_params=_cparams(("arbitrary",), 40),
        name="final_norm",
    )(h, w.reshape(1, D))


def _mm_body(x_ref, w_ref, o_ref, wb_ref):
    @pl.when(pl.program_id(1) == 0)
    def _():
        wb_ref[...] = w_ref[...].astype(BF16)

    o_ref[...] = _dot(x_ref[...], wb_ref[...]).astype(o_ref.dtype)


def _matmul(x, w, tm, tn, out_dtype=BF16):
    T, K = x.shape
    N = w.shape[1]
    return pl.pallas_call(
        _mm_body,
        out_shape=jax.ShapeDtypeStruct((T, N), out_dtype),
        grid=(N // tn, T // tm),
        in_specs=[
            pl.BlockSpec((tm, K), lambda j, i: (i, 0)),
            pl.BlockSpec((K, tn), lambda j, i: (0, j)),
        ],
        out_specs=pl.BlockSpec((tm, tn), lambda j, i: (i, j)),
        scratch_shapes=[pltpu.VMEM((K, tn), BF16)],
        compiler_params=_cparams(("arbitrary", "arbitrary"), 56),
        name="matmul",
    )(x, w)


def _mm_res_body(x_ref, w_ref, h_ref, mod_ref, o_ref, wb_ref, *, gate_idx):
    @pl.when(pl.program_id(1) == 0)
    def _():
        wb_ref[...] = w_ref[...].astype(BF16)

    gate = mod_ref[0, pl.ds(gate_idx, 1), :]
    o_ref[...] = h_ref[...] + gate * _dot(x_ref[...], wb_ref[...])


def _matmul_residual(x, w, h, mod, group_of_tile, tm, tn, gate_idx):
    T, K = x.shape
    N = w.shape[1]
    return pl.pallas_call(
        functools.partial(_mm_res_body, gate_idx=gate_idx),
        out_shape=jax.ShapeDtypeStruct((T, N), F32),
        grid=(N // tn, T // tm),
        in_specs=[
            pl.BlockSpec((tm, K), lambda j, i: (i, 0)),
            pl.BlockSpec((K, tn), lambda j, i: (0, j)),
            pl.BlockSpec((tm, tn), lambda j, i: (i, j)),
            pl.BlockSpec((1, N_MOD, tn), lambda j, i: (group_of_tile(i), 0, j)),
        ],
        out_specs=pl.BlockSpec((tm, tn), lambda j, i: (i, j)),
        scratch_shapes=[pltpu.VMEM((K, tn), BF16)],
        compiler_params=_cparams(("arbitrary", "arbitrary"), 56),
        name="matmul_residual",
    )(x, w, h, mod)


def _retention_body(lg_ref, qx_ref, kx_ref, vx_ref, gx_ref, qc_ref, kc_ref, vc_ref, gc_ref,
                    cos_ref, sin_ref, gnw_ref, ox_ref, oc_ref, qr_ref, kr_ref, sb_ref,
                    *, nch_x, nch_c, k_scale):
    C = RET_CHUNK
    head = pl.program_id(1)
    lgf = lg_ref[0, head]
    lgb = lg_ref[1, head]

    ri = lax.broadcasted_iota(jnp.int32, (C, C), 0)
    ci = lax.broadcasted_iota(jnp.int32, (C, C), 1)
    diff = (ri - ci).astype(F32)
    dmask = (jnp.where(diff >= 0.0, jnp.exp(lgf * jnp.maximum(diff, 0.0)), 0.0)
             + jnp.where(diff <= 0.0, jnp.exp(lgb * jnp.maximum(-diff, 0.0)), 0.0))
    pos = lax.broadcasted_iota(jnp.int32, (C, 1), 0).astype(F32)
    q_dec_f = jnp.exp(lgf * (pos + 1.0))
    k_dec_f = jnp.exp(lgf * (C - 1.0 - pos))
    q_dec_b = jnp.exp(lgb * (C - pos))
    k_dec_b = jnp.exp(lgb * pos)
    full_c = jnp.full((C, C), float(C), F32)
    chunk_dec_f = jnp.exp(lgf * full_c)
    chunk_dec_b = jnp.exp(lgb * full_c)
    swap = jnp.where((ri ^ 32) == ci, 1.0, 0.0).astype(BF16)
    gnw = gnw_ref[...]

    def rope_chunk(c, carry):
        rows = pl.ds(pl.multiple_of(c * C, C), C)
        cos = cos_ref[rows, :]
        sin = sin_ref[rows, :]
        q = qx_ref[rows, :]
        k = kx_ref[rows, :]
        qr_ref[rows, :] = (q.astype(F32) * cos + _dot(q, swap) * sin).astype(BF16)
        kr_ref[rows, :] = ((k.astype(F32) * cos + _dot(k, swap) * sin) * k_scale).astype(BF16)
        return carry

    lax.fori_loop(0, nch_x, rope_chunk, 0)

    def segment(q_at, k_at, v_at, g_ref, o_ref, nch, s_f0, s_b0):
        def bwd(t, s):
            c = nch - 1 - t
            sb_ref[c] = s.astype(BF16)
            kd = (k_at(c).astype(F32) * k_dec_b).astype(BF16)
            return chunk_dec_b * s + lax.dot_general(kd, v_at(c), (((0,), (0,)), ((), ())),
                                                     preferred_element_type=F32)

        s_b_final = lax.fori_loop(0, nch, bwd, s_b0)

        def fwd(c, s):
            rows = pl.ds(pl.multiple_of(c * C, C), C)
            q = q_at(c)
            k = k_at(c)
            v = v_at(c)
            qf = q.astype(F32)
            scores = lax.dot_general(q, k, (((1,), (1,)), ((), ())), preferred_element_type=F32) * dmask
            y = (_dot(scores.astype(BF16), v)
                 + _dot((qf * q_dec_f).astype(BF16), s.astype(BF16))
                 + _dot((qf * q_dec_b).astype(BF16), sb_ref[c]))
            kd = (k.astype(F32) * k_dec_f).astype(BF16)
            s_new = chunk_dec_f * s + lax.dot_general(kd, v, (((0,), (0,)), ((), ())),
                                                      preferred_element_type=F32)
            mu = jnp.mean(y, axis=-1, keepdims=True)
            d = y - mu
            var = jnp.mean(d * d, axis=-1, keepdims=True)
            yn = d * lax.rsqrt(var + EPS) * gnw
            g = g_ref[rows, :].astype(F32)
            o_ref[rows, :] = (g * _sigmoid(g) * yn).astype(o_ref.dtype)
            return s_new

        s_f_final = lax.fori_loop(0, nch, fwd, s_f0)
        return s_f_final, s_b_final

    def at(ref, scale=None):
        def get(c):
            x = ref[pl.ds(pl.multiple_of(c * C, C), C), :]
            if scale is not None:
                x = (x.astype(F32) * scale).astype(BF16)
            return x
        return get

    zero = jnp.zeros((C, C), F32)
    s_f, s_b = segment(at(qc_ref), at(kc_ref, k_scale), at(vc_ref), gc_ref, oc_ref, nch_c, zero, zero)
    segment(at(qr_ref), at(kr_ref), at(vx_ref), gx_ref, ox_ref, nch_x, s_f, s_b)


def _retention(z, log_gamma, cos_t, sin_t, gn_w, B, n_lat, n_ctx):
    W = N_RET_HEADS * LANES
    H = N_RET_HEADS
    ctx0 = (B * n_lat) // n_ctx

    def xspec(col):
        return pl.BlockSpec((n_lat, LANES), lambda b, h: (b, col * H + h))

    def cspec(col):
        return pl.BlockSpec((n_ctx, LANES), lambda b, h: (ctx0 + b, col * H + h))

    nch_x = n_lat // RET_CHUNK
    nch_c = n_ctx // RET_CHUNK
    return pl.pallas_call(
        functools.partial(_retention_body, nch_x=nch_x, nch_c=nch_c, k_scale=float(LANES) ** -0.5),
        out_shape=(jax.ShapeDtypeStruct((B * n_lat, W), BF16), jax.ShapeDtypeStruct((B * n_ctx, W), BF16)),
        grid=(B, H),
        in_specs=[
            pl.BlockSpec(memory_space=pltpu.SMEM),
            xspec(0), xspec(1), xspec(2), xspec(3),
            cspec(0), cspec(1), cspec(2), cspec(3),
            pl.BlockSpec((n_lat, LANES), lambda b, h: (0, 0)),
            pl.BlockSpec((n_lat, LANES), lambda b, h: (0, 0)),
            pl.BlockSpec((1, LANES), lambda b, h: (0, h)),
        ],
        out_specs=(pl.BlockSpec((n_lat, LANES), lambda b, h: (b, h)),
                   pl.BlockSpec((n_ctx, LANES), lambda b, h: (b, h))),
        scratch_shapes=[pltpu.VMEM((n_lat, LANES), BF16), pltpu.VMEM((n_lat, LANES), BF16),
                        pltpu.VMEM((max(nch_x, nch_c), RET_CHUNK, RET_CHUNK), BF16)],
        compiler_params=_cparams(("arbitrary", "arbitrary"), 48),
        name="retention",
    )(log_gamma, z, z, z, z, z, z, z, z, cos_t, sin_t, gn_w.reshape(1, W))


def _branch_body(first_ref, last_ref, u_ref, v_ref, pp_ref, pc_ref, pn_ref, sgn_ref, sgw_ref, sgb_ref,
                 pw_ref, pb_ref, ps_ref, yb_ref, yc_ref):
    C = SG_CHUNK
    i = pl.program_id(0)
    is_first = first_ref[i] == 1
    is_last = last_ref[i] == 1

    u = _gelu_tanh(u_ref[...].astype(F32))
    v = _gelu_tanh(v_ref[...].astype(F32))
    mu = jnp.mean(v, axis=-1, keepdims=True)
    d = v - mu
    var = jnp.mean(d * d, axis=-1, keepdims=True)
    vn = (d * lax.rsqrt(var + EPS) * sgn_ref[...]).astype(BF16)
    for g in range(SG_GROUPS):
        cols = slice(g * LANES, (g + 1) * LANES)
        mixed = _dot(sgw_ref[g], vn[:, cols]) + sgb_ref[:, cols]
        yb_ref[:, cols] = (u[:, cols] * mixed).astype(yb_ref.dtype)

    r = lax.broadcasted_iota(jnp.int32, (C, C), 0)
    s = lax.broadcasted_iota(jnp.int32, (C, C), 1)
    rr = lax.broadcasted_iota(jnp.int32, (C, 1), 0)
    gw = pc_ref.shape[1] // len(POOL_WINDOWS)
    for gi, w in enumerate(POOL_WINDOWS):
        hw = w // 2
        cols = slice(gi * gw, (gi + 1) * gw)
        a_mid = jnp.where((s >= r - hw) & (s < r + hw), 1.0, 0.0)
        a_prev = jnp.where(is_first, 0.0, jnp.where(s >= r - hw + C, 1.0, 0.0))
        a_next = jnp.where(is_last, 0.0, jnp.where(s < r + hw - C, 1.0, 0.0))
        lo = jnp.where(is_first, jnp.maximum(rr - hw, 0), rr - hw)
        hi = jnp.where(is_last, jnp.minimum(rr + hw, C), rr + hw)
        cnt = (hi - lo).astype(F32)
        pg = pc_ref[:, cols]
        tot = (_dot(a_mid.astype(BF16), pg) + _dot(a_prev.astype(BF16), pp_ref[:, cols])
               + _dot(a_next.astype(BF16), pn_ref[:, cols]))
        pooled = tot / cnt - pg.astype(F32)
        yc = (_dot(pooled.astype(BF16), pw_ref[gi]) + pb_ref[:, cols]) * ps_ref[:, cols]
        yc_ref[:, cols] = yc.astype(yc_ref.dtype)


def _branches_bc(z, first_flag, last_flag, sg_norm_w, sg_w, sgb_full, pool_w, pool_b, pool_scale):
    T = z.shape[0]
    W = N_RET_HEADS * LANES
    C = SG_CHUNK
    nblk = T // C
    row = lambda col: pl.BlockSpec((C, W), lambda i, f, l: (i, col))
    const2 = lambda shape: pl.BlockSpec(shape, lambda i, f, l: (0, 0))
    const3 = lambda shape: pl.BlockSpec(shape, lambda i, f, l: (0, 0, 0))
    grid_spec = pltpu.PrefetchScalarGridSpec(
        num_scalar_prefetch=2,
        grid=(nblk,),
        in_specs=[
            row(4), row(5),
            pl.BlockSpec((C, W), lambda i, f, l: (jnp.maximum(i - 1, 0), 6)),
            row(6),
            pl.BlockSpec((C, W), lambda i, f, l: (jnp.minimum(i + 1, nblk - 1), 6)),
            const2((1, W)), const3((SG_GROUPS, C, C)), const2((C, W)),
            const3(pool_w.shape), const2((1, W)), const2((1, W)),
        ],
        out_specs=(pl.BlockSpec((C, W), lambda i, f, l: (i, 0)), pl.BlockSpec((C, W), lambda i, f, l: (i, 0))),
    )
    return pl.pallas_call(
        _branch_body,
        out_shape=(jax.ShapeDtypeStruct((T, W), BF16), jax.ShapeDtypeStruct((T, W), BF16)),
        grid_spec=grid_spec,
        compiler_params=_cparams(("arbitrary",), 32),
        name="branches_bc",
    )(first_flag, last_flag, z, z, z, z, z, sg_norm_w.reshape(1, W), sg_w.astype(BF16), sgb_full,
      pool_w.astype(BF16), pool_b.reshape(1, W), pool_scale.reshape(1, W))


def _merge_body(r_ref, ya_ref, yb_ref, yc_ref, gu_ref, gb_ref, wbr_ref, o_ref, gub_ref, wbb_ref):
    @pl.when(pl.program_id(1) == 0)
    def _():
        gub_ref[...] = gu_ref[...].astype(BF16)
        wbb_ref[...] = wbr_ref[...].astype(BF16)

    r = r_ref[...]
    acc = None
    for k, y_ref in enumerate((ya_ref, yb_ref, yc_ref)):
        gate = _sigmoid(_dot(r, gub_ref[k]) + gb_ref[k])
        term = gate * _dot(y_ref[...], wbb_ref[k])
        acc = term if acc is None else acc + term
    o_ref[...] = acc.astype(o_ref.dtype)


def _merge(r, ya, yb, yc, gate_up, gate_b, w_branch, tm, tn):
    T, R = r.shape
    W = ya.shape[1]
    D = gate_up.shape[2]
    yspec = pl.BlockSpec((tm, W), lambda j, i: (i, 0))
    return pl.pallas_call(
        _merge_body,
        out_shape=jax.ShapeDtypeStruct((T, D), BF16),
        grid=(D // tn, T // tm),
        in_specs=[
            pl.BlockSpec((tm, R), lambda j, i: (i, 0)),
            yspec, yspec, yspec,
            pl.BlockSpec((3, R, tn), lambda j, i: (0, 0, j)),
            pl.BlockSpec((3, 1, tn), lambda j, i: (0, 0, j)),
            pl.BlockSpec((3, W, tn), lambda j, i: (0, 0, j)),
        ],
        out_specs=pl.BlockSpec((tm, tn), lambda j, i: (i, j)),
        scratch_shapes=[pltpu.VMEM((3, R, tn), BF16), pltpu.VMEM((3, W, tn), BF16)],
        compiler_params=_cparams(("arbitrary", "arbitrary"), 56),
        name="merge",
    )(r, ya, yb, yc, gate_up, gate_b.reshape(3, 1, D), w_branch)


def _router_body(x_ref, w_ref, b_ref, o_ref):
    logits = _dot(x_ref[...], w_ref[...]) + b_ref[...]
    lane = lax.broadcasted_iota(jnp.int32, logits.shape, 1)
    work = jnp.where(lane < N_EXPERTS, logits, NEG_BIG)
    vals, hots = [], []
    for _ in range(TOP_K):
        m = jnp.max(work, axis=-1, keepdims=True)
        idx = jnp.min(jnp.where(work == m, lane, LANES), axis=-1, keepdims=True)
        hot = lane == idx
        vals.append(m)
        hots.append(hot)
        work = jnp.where(hot, NEG_BIG, work)
    es = [jnp.exp(v - vals[0]) for v in vals]
    inv = 1.0 / (es[0] + es[1] + es[2] + es[3])
    comb = jnp.zeros(logits.shape, F32)
    for e, hot in zip(es, hots):
        comb = comb + jnp.where(hot, e * inv, 0.0)
    o_ref[...] = comb


def _router(x, router_w, router_b, tm):
    T, D = x.shape
    E = router_w.shape[1]
    wp = jnp.zeros((D, LANES), BF16).at[:, :E].set(router_w.astype(BF16))
    bp = jnp.zeros((1, LANES), F32).at[0, :E].set(router_b)
    return pl.pallas_call(
        _router_body,
        out_shape=jax.ShapeDtypeStruct((T, LANES), F32),
        grid=(T // tm,),
        in_specs=[pl.BlockSpec((tm, D), lambda i: (i, 0)), pl.BlockSpec((D, LANES), lambda i: (0, 0)),
                  pl.BlockSpec((1, LANES), lambda i: (0, 0))],
        out_specs=pl.BlockSpec((tm, LANES), lambda i: (i, 0)),
        compiler_params=_cparams(("arbitrary",), 32),
        name="router",
    )(x, wp, bp)


def _split_bf16(x):
    hi = x.astype(BF16)
    lo = (x - hi.astype(F32)).astype(BF16)
    return hi, lo


def _moe_up_body(x_ref, comb_ref, wg_ref, wl_ref, bg_ref, bl_ref, o_ref, *, d_expert):
    x = x_ref[...]
    tn = o_ref.shape[1]
    glu = jnp.minimum(_dot(x, wg_ref[...]) + bg_ref[...], SWIGLU_LIMIT)
    lin = jnp.clip(_dot(x, wl_ref[...]) + bl_ref[...], -SWIGLU_LIMIT, SWIGLU_LIMIT)
    e = lax.broadcasted_iota(jnp.int32, (LANES, tn), 0)
    col = lax.broadcasted_iota(jnp.int32, (LANES, tn), 1) + pl.program_id(0) * tn
    expand = jnp.where((col >= e * d_expert) & (col < (e + 1) * d_expert), 1.0, 0.0).astype(BF16)
    hi, lo = _split_bf16(comb_ref[...])
    cw = _dot(hi, expand) + _dot(lo, expand)
    o_ref[...] = (glu * _sigmoid(SWIGLU_ALPHA * glu) * (lin + 1.0) * cw).astype(o_ref.dtype)


def _moe_up(x, comb, wg, wl, bg, bl, d_expert, tm, tn):
    T, D = x.shape
    N = wg.shape[1]
    wspec = pl.BlockSpec((D, tn), lambda j, i: (0, j))
    bspec = pl.BlockSpec((1, tn), lambda j, i: (0, j))
    return pl.pallas_call(
        functools.partial(_moe_up_body, d_expert=d_expert),
        out_shape=jax.ShapeDtypeStruct((T, N), BF16),
        grid=(N // tn, T // tm),
        in_specs=[pl.BlockSpec((tm, D), lambda j, i: (i, 0)), pl.BlockSpec((tm, LANES), lambda j, i: (i, 0)),
                  wspec, wspec, bspec, bspec],
        out_specs=pl.BlockSpec((tm, tn), lambda j, i: (i, j)),
        compiler_params=_cparams(("arbitrary", "arbitrary"), 56),
        name="moe_up",
    )(x, comb, wg, wl, bg, bl)


def _moe_down_body(a_ref, w_ref, comb_ref, b2_ref, h_ref, mod_ref, o_ref, wb_ref, *, gate_idx):
    @pl.when(pl.program_id(1) == 0)
    def _():
        wb_ref[...] = w_ref[...].astype(BF16)

    hi, lo = _split_bf16(comb_ref[...])
    b2 = b2_ref[...]
    y = _dot(a_ref[...], wb_ref[...]) + _dot(hi, b2) + _dot(lo, b2)
    gate = mod_ref[0, pl.ds(gate_idx, 1), :]
    o_ref[...] = h_ref[...] + gate * y


def _moe_down(act, w2, comb, b2p, h, mod, group_of_tile, tm, tn, gate_idx):
    T, K = act.shape
    N = w2.shape[1]
    return pl.pallas_call(
        functools.partial(_moe_down_body, gate_idx=gate_idx),
        out_shape=jax.ShapeDtypeStruct((T, N), F32),
        grid=(N // tn, T // tm),
        in_specs=[
            pl.BlockSpec((tm, K), lambda j, i: (i, 0)),
            pl.BlockSpec((K, tn), lambda j, i: (0, j)),
            pl.BlockSpec((tm, LANES), lambda j, i: (i, 0)),
            pl.BlockSpec((LANES, tn), lambda j, i: (0, j)),
            pl.BlockSpec((tm, tn), lambda j, i: (i, j)),
            pl.BlockSpec((1, N_MOD, tn), lambda j, i: (group_of_tile(i), 0, j)),
        ],
        out_specs=pl.BlockSpec((tm, tn), lambda j, i: (i, j)),
        scratch_shapes=[pltpu.VMEM((K, tn), BF16)],
        compiler_params=_cparams(("arbitrary", "arbitrary"), 56),
        name="moe_down",
    )(act, w2, comb, b2p, h, mod)


@jax.jit
def _forward(x, c, ctx, c_ctx, ada_w, ada_b, norm1_w, norm2_w, w_in, ret_decay, ret_gn_w,
             sg_norm_w, sg_w, sg_b, pool_w, pool_b, pool_scale, w_branch, gate_down, gate_up,
             gate_b, w_out, router_w, router_b, w1, b1, w2, b2, final_norm_w):
    B, n_lat, D = x.shape
    n_ctx = ctx.shape[1]
    depth = ada_w.shape[0]
    W = N_RET_HEADS * LANES
    E, _, F2 = w1.shape[1:]
    F = F2 // 2
    T = B * (n_lat + n_ctx)
    assert B + 1 <= COND_ROWS and n_lat % GRID_W == 0
    assert n_lat % RET_CHUNK == 0 and n_ctx % RET_CHUNK == 0 and (B * n_lat) % n_ctx == 0

    tg = _pick(math.gcd(n_lat, B * n_ctx), (512, 256, 128))
    group_of_tile = lambda i: jnp.minimum((i * tg) // n_lat, B)
    tm = _pick(T, (1088, 1024, 768, 512, 384, 256, 128))

    h = jnp.concatenate([x.reshape(B * n_lat, D), ctx.reshape(B * n_ctx, D)], axis=0)
    cond = jnp.zeros((COND_ROWS, D), F32).at[:B].set(c).at[B].set(c_ctx)
    mods = _adaln(cond, ada_w, ada_b).reshape(depth, COND_ROWS, N_MOD, D)

    t = jnp.arange(n_lat)
    n_freq = LANES // 4
    freqs = ROPE_BASE ** (-jnp.arange(n_freq, dtype=F32) / n_freq)
    ang_r = (t // GRID_W).astype(F32)[:, None] * freqs
    ang_c = (t % GRID_W).astype(F32)[:, None] * freqs
    cos_t = jnp.concatenate([jnp.cos(ang_r)] * 2 + [jnp.cos(ang_c)] * 2, axis=-1)
    sin_t = jnp.concatenate([-jnp.sin(ang_r), jnp.sin(ang_r), -jnp.sin(ang_c), jnp.sin(ang_c)], axis=-1)

    blk = jnp.arange(T // SG_CHUNK)
    lat_blk, ctx_blk = n_lat // SG_CHUNK, n_ctx // SG_CHUNK
    in_lat = blk < B * lat_blk
    local = jnp.where(in_lat, blk % lat_blk, (blk - B * lat_blk) % ctx_blk)
    seq_blk = jnp.where(in_lat, lat_blk, ctx_blk)
    first_flag = (local == 0).astype(jnp.int32)
    last_flag = (local == seq_blk - 1).astype(jnp.int32)

    log_gamma = jax.nn.log_sigmoid(ret_decay.astype(F32))

    for l in range(depth):
        mod = mods[l]
        n1 = _norm_mod(h, norm1_w[l], mod, group_of_tile, tg, 0, 1)
        z = _matmul(n1, w_in[l], tm, 512)
        r = _matmul(n1, gate_down[l], tm, gate_down.shape[2])
        ya_x, ya_c = _retention(z, log_gamma[l], cos_t, sin_t, ret_gn_w[l], B, n_lat, n_ctx)
        ya = jnp.concatenate([ya_x, ya_c], axis=0)
        sgb_full = jnp.repeat(sg_b[l].T, LANES, axis=1)
        yb, yc = _branches_bc(z, first_flag, last_flag, sg_norm_w[l], sg_w[l], sgb_full,
                              pool_w[l], pool_b[l], pool_scale[l])
        m = _merge(r, ya, yb, yc, gate_up[l], gate_b[l], w_branch[l], tm, 512)
        h = _matmul_residual(m, w_out[l], h, mod, group_of_tile, tg, 512, 2)
        n2 = _norm_mod(h, norm2_w[l], mod, group_of_tile, tg, 3, 4)
        comb = _router(n2, router_w[l], router_b[l], tg)
        wg = jnp.transpose(w1[l][:, :, 0::2], (1, 0, 2)).reshape(D, E * F).astype(BF16)
        wl = jnp.transpose(w1[l][:, :, 1::2], (1, 0, 2)).reshape(D, E * F).astype(BF16)
        bg = b1[l][:, 0::2].reshape(1, E * F)
        bl = b1[l][:, 1::2].reshape(1, E * F)
        act = _moe_up(n2, comb, wg, wl, bg, bl, F, tm, 4 * F)
        b2p = jnp.zeros((LANES, D), BF16).at[:E].set(b2[l].astype(BF16))
        h = _moe_down(act, w2[l].reshape(E * F, D), comb, b2p, h, mod, group_of_tile, tg, 512, 5)

    out = _final_norm(h, final_norm_w, B * n_lat, tg)
    return out.reshape(B, n_lat, D)


def kernel(x, c, ctx, c_ctx, ada_w, ada_b, norm1_w, norm2_w, w_in, ret_decay, ret_gn_w, sg_norm_w, sg_w, sg_b, pool_w, pool_b, pool_scale, w_branch, gate_down, gate_up, gate_b, w_out, router_w, router_b, w1, b1, w2, b2, final_norm_w):
    return _forward(x, c, ctx, c_ctx, ada_w, ada_b, norm1_w, norm2_w, w_in, ret_decay, ret_gn_w,
                    sg_norm_w, sg_w, sg_b, pool_w, pool_b, pool_scale, w_branch, gate_down, gate_up,
                    gate_b, w_out, router_w, router_b, w1, b1, w2, b2, final_norm_w)
```

```python
import functools
import math

import jax
import jax.numpy as jnp
import numpy as np
from jax import lax
from jax.experimental import pallas as pl
from jax.experimental.pallas import tpu as pltpu

F32 = jnp.float32
BF16 = jnp.bfloat16
U32 = jnp.uint32
I32 = jnp.int32

GRID_W = 64
N_RET_HEADS = 8
RET_CHUNK = 128
ROPE_BASE = 10000.0
SG_CHUNK = 128
SG_GROUPS = 8
POOL_WINDOWS = (2, 4, 8, 16)
N_EXPERTS = 32
TOP_K = 4
SWIGLU_LIMIT = 7.0
SWIGLU_ALPHA = 1.702
N_MOD = 6
EPS = 1e-6

LANES = 128
SUBLANES = 8
MXU_DIM = 256
COND_ROWS = 8
NEG_BIG = -1e30
EXPERT_TILE = 256
COMBINE_TILE = 256
MIB = 1024 * 1024


def _cparams(sem, vmem_mib):
    return pltpu.CompilerParams(dimension_semantics=sem, vmem_limit_bytes=int(vmem_mib * MIB))


def _pick(n, cands):
    for c in cands:
        if n % c == 0:
            return c
    raise ValueError(f"no tile in {cands} divides {n}")


def _dot(a, b):
    return jnp.dot(a, b, preferred_element_type=F32)


def _dot_tn(a, b):
    return lax.dot_general(a, b, (((0,), (0,)), ((), ())), preferred_element_type=F32)


def _dot_nt(a, b):
    return lax.dot_general(a, b, (((1,), (1,)), ((), ())), preferred_element_type=F32)


def _sigmoid(x):
    return 1.0 / (1.0 + jnp.exp(-x))


def _gelu_tanh(x):
    return 0.5 * x * (1.0 + jnp.tanh(math.sqrt(2.0 / math.pi) * (x + 0.044715 * (x * x * x))))


def _pack_bf16_pairs(x):
    n = x.shape[1] // 2
    xb = x.astype(BF16).astype(F32)
    lo = lax.shift_right_logical(lax.bitcast_convert_type(xb[:, :n], U32), jnp.uint32(16))
    hi = lax.bitcast_convert_type(xb[:, n:], U32) & jnp.uint32(0xFFFF0000)
    return lo | hi


def _unpack_bf16_pairs(w):
    lo = lax.bitcast_convert_type(lax.shift_left(w, jnp.uint32(16)), F32)
    hi = lax.bitcast_convert_type(w & jnp.uint32(0xFFFF0000), F32)
    return lo, hi


def _adaln_body(cond_ref, w_ref, b_ref, o_ref):
    s = cond_ref[...]
    s = (s * _sigmoid(s)).astype(BF16)
    o_ref[0] = _dot(s, w_ref[0].astype(BF16)) + b_ref[0]


def _adaln(cond, ada_w, ada_b):
    L, D, M = ada_w.shape
    tn = _pick(M, (512, 256, 128))
    return pl.pallas_call(
        _adaln_body,
        out_shape=jax.ShapeDtypeStruct((L, COND_ROWS, M), F32),
        grid=(L, M // tn),
        in_specs=[
            pl.BlockSpec((COND_ROWS, D), lambda l, j: (0, 0)),
            pl.BlockSpec((1, D, tn), lambda l, j: (l, 0, j)),
            pl.BlockSpec((1, 1, tn), lambda l, j: (l, 0, j)),
        ],
        out_specs=pl.BlockSpec((1, COND_ROWS, tn), lambda l, j: (l, 0, j)),
        compiler_params=_cparams(("arbitrary", "arbitrary"), 40),
        name="adaln",
    )(cond, ada_w, ada_b.reshape(L, 1, M))


def _rms_modulate(x, w, mod_ref, layer, shift_idx, scale_idx):
    ms = jnp.mean(x * x, axis=-1, keepdims=True)
    y = x * lax.rsqrt(ms + EPS) * w
    shift = mod_ref[0, 0, pl.ds(shift_idx, 1), :]
    scale = mod_ref[0, 0, pl.ds(scale_idx, 1), :]
    return y * (1.0 + scale) + shift


def _norm_mod_body(h_ref, w_ref, mod_ref, o_ref, *, layer, shift_idx, scale_idx):
    o_ref[...] = _rms_modulate(h_ref[...], w_ref[0], mod_ref, layer, shift_idx, scale_idx).astype(o_ref.dtype)


def _norm_mod(h, w, mods, layer, group_of_tile, tm, shift_idx, scale_idx):
    T, D = h.shape
    return pl.pallas_call(
        functools.partial(_norm_mod_body, layer=layer, shift_idx=shift_idx, scale_idx=scale_idx),
        out_shape=jax.ShapeDtypeStruct((T, D), BF16),
        grid=(T // tm,),
        in_specs=[
            pl.BlockSpec((tm, D), lambda i: (i, 0)),
            pl.BlockSpec((1, 1, D), lambda i: (layer, 0, 0)),
            pl.BlockSpec((1, 1, N_MOD, D), lambda i: (layer, group_of_tile(i), 0, 0)),
        ],
        out_specs=pl.BlockSpec((tm, D), lambda i: (i, 0)),
        compiler_params=_cparams(("arbitrary",), 40),
        name="norm_mod",
    )(h, w, mods)


def _final_norm_body(h_ref, w_ref, o_ref):
    x = h_ref[...]
    ms = jnp.mean(x * x, axis=-1, keepdims=True)
    o_ref[...] = x * lax.rsqrt(ms + EPS) * w_ref[...]


def _final_norm(h, w, rows, tm):
    D = h.shape[1]
    return pl.pallas_call(
        _final_norm_body,
        out_shape=jax.ShapeDtypeStruct((rows, D), F32),
        grid=(rows // tm,),
        in_specs=[pl.BlockSpec((tm, D), lambda i: (i, 0)), pl.BlockSpec((1, D), lambda i: (0, 0))],
        out_specs=pl.BlockSpec((tm, D), lambda i: (i, 0)),
        compiler_params=_cparams(("arbitrary",), 40),
        name="final_norm",
    )(h, w.reshape(1, D))


def _mm_body(x_ref, w_ref, o_ref, wb_ref):
    @pl.when(pl.program_id(1) == 0)
    def _():
        wb_ref[...] = w_ref[0].astype(BF16)

    o_ref[...] = _dot(x_ref[...], wb_ref[...]).astype(o_ref.dtype)


def _matmul(x, w, layer, tm, tn, out_dtype=BF16):
    T, K = x.shape
    N = w.shape[2]
    return pl.pallas_call(
        _mm_body,
        out_shape=jax.ShapeDtypeStruct((T, N), out_dtype),
        grid=(N // tn, T // tm),
        in_specs=[
            pl.BlockSpec((tm, K), lambda j, i: (i, 0)),
            pl.BlockSpec((1, K, tn), lambda j, i: (layer, 0, j)),
        ],
        out_specs=pl.BlockSpec((tm, tn), lambda j, i: (i, j)),
        scratch_shapes=[pltpu.VMEM((K, tn), BF16)],
        compiler_params=_cparams(("arbitrary", "arbitrary"), 56),
        name="matmul",
    )(x, w)


def _mm_res_body(x_ref, w_ref, h_ref, mod_ref, o_ref, wb_ref, *, gate_idx):
    @pl.when(pl.program_id(1) == 0)
    def _():
        wb_ref[...] = w_ref[0].astype(BF16)

    gate = mod_ref[0, 0, pl.ds(gate_idx, 1), :]
    o_ref[...] = h_ref[...] + gate * _dot(x_ref[...], wb_ref[...])


def _matmul_residual(x, w, h, mods, layer, group_of_tile, tm, tn, gate_idx):
    T, K = x.shape
    N = w.shape[2]
    return pl.pallas_call(
        functools.partial(_mm_res_body, gate_idx=gate_idx),
        out_shape=jax.ShapeDtypeStruct((T, N), F32),
        grid=(N // tn, T // tm),
        in_specs=[
            pl.BlockSpec((tm, K), lambda j, i: (i, 0)),
            pl.BlockSpec((1, K, tn), lambda j, i: (layer, 0, j)),
            pl.BlockSpec((tm, tn), lambda j, i: (i, j)),
            pl.BlockSpec((1, 1, N_MOD, tn), lambda j, i: (layer, group_of_tile(i), 0, j)),
        ],
        out_specs=pl.BlockSpec((tm, tn), lambda j, i: (i, j)),
        scratch_shapes=[pltpu.VMEM((K, tn), BF16)],
        compiler_params=_cparams(("arbitrary", "arbitrary"), 56),
        name="matmul_residual",
    )(x, w, h, mods)


def _retention_body(lg_ref, qx_ref, kx_ref, vx_ref, gx_ref, qc_ref, kc_ref, vc_ref, gc_ref,
                    cos_ref, sin_ref, gnw_ref, ox_ref, oc_ref, qr_ref, kr_ref, u_ref, st_ref,
                    *, layer, nch_x, nch_c, k_scale):
    C = RET_CHUNK
    head = pl.program_id(1)
    lgf = lg_ref[layer, 0, head]
    lgb = lg_ref[layer, 1, head]

    ri = lax.broadcasted_iota(I32, (C, C), 0)
    ci = lax.broadcasted_iota(I32, (C, C), 1)
    diff = (ri - ci).astype(F32)
    dmask = (jnp.where(diff >= 0.0, jnp.exp(lgf * jnp.maximum(diff, 0.0)), 0.0)
             + jnp.where(diff <= 0.0, jnp.exp(lgb * jnp.maximum(-diff, 0.0)), 0.0))
    pos = lax.broadcasted_iota(I32, (C, 1), 0).astype(F32)
    q_dec_f = jnp.exp(lgf * (pos + 1.0))
    k_dec_f = jnp.exp(lgf * (C - 1.0 - pos))
    q_dec_b = jnp.exp(lgb * (C - pos))
    k_dec_b = jnp.exp(lgb * pos)
    full_c = jnp.full((C, C), float(C), F32)
    chunk_dec_f = jnp.exp(lgf * full_c)
    chunk_dec_b = jnp.exp(lgb * full_c)
    swap = jnp.where((ri ^ 32) == ci, 1.0, 0.0).astype(BF16)
    gnw = gnw_ref[0]

    def rows_of(c):
        return pl.ds(pl.multiple_of(c * C, C), C)

    def rope_chunk(c, carry):
        rows = rows_of(c)
        cos = cos_ref[rows, :]
        sin = sin_ref[rows, :]
        q = qx_ref[rows, :]
        k = kx_ref[rows, :]
        qr_ref[rows, :] = (q.astype(F32) * cos + _dot(q, swap) * sin).astype(BF16)
        kr_ref[rows, :] = ((k.astype(F32) * cos + _dot(k, swap) * sin) * k_scale).astype(BF16)
        return carry

    lax.fori_loop(0, nch_x, rope_chunk, 0, unroll=2)

    def segment(q_at, k_at, v_at, g_ref, o_ref, nch, s_f0, s_b0):
        def incr(c, carry):
            kf = k_at(c).astype(F32)
            kd = jnp.concatenate([kf * k_dec_f, kf * k_dec_b], axis=1).astype(BF16)
            u_ref[c] = _dot_tn(kd, v_at(c))
            return carry

        lax.fori_loop(0, nch, incr, 0, unroll=2)

        def fwd_scan(c, s):
            st_ref[c, 0:C, :] = s.astype(BF16)
            return chunk_dec_f * s + u_ref[c, 0:C, :]

        s_f_final = lax.fori_loop(0, nch, fwd_scan, s_f0)

        def bwd_scan(t, s):
            c = nch - 1 - t
            st_ref[c, C:2 * C, :] = s.astype(BF16)
            return chunk_dec_b * s + u_ref[c, C:2 * C, :]

        s_b_final = lax.fori_loop(0, nch, bwd_scan, s_b0)

        def emit(c, carry):
            rows = rows_of(c)
            q = q_at(c)
            v = v_at(c)
            qf = q.astype(F32)
            scores = _dot_nt(q, k_at(c)) * dmask
            qd = jnp.concatenate([qf * q_dec_f, qf * q_dec_b], axis=1).astype(BF16)
            y = _dot(scores.astype(BF16), v) + _dot(qd, st_ref[c])
            mu = jnp.mean(y, axis=-1, keepdims=True)
            d = y - mu
            var = jnp.mean(d * d, axis=-1, keepdims=True)
            yn = d * lax.rsqrt(var + EPS) * gnw
            g = g_ref[rows, :].astype(F32)
            o_ref[rows, :] = (g * _sigmoid(g) * yn).astype(o_ref.dtype)
            return carry

        lax.fori_loop(0, nch, emit, 0, unroll=2)
        return s_f_final, s_b_final

    def at(ref, scale=None):
        def get(c):
            x = ref[rows_of(c), :]
            if scale is not None:
                x = (x.astype(F32) * scale).astype(BF16)
            return x
        return get

    zero = jnp.zeros((C, C), F32)
    s_f, s_b = segment(at(qc_ref), at(kc_ref, k_scale), at(vc_ref), gc_ref, oc_ref, nch_c, zero, zero)
    segment(at(qr_ref), at(kr_ref), at(vx_ref), gx_ref, ox_ref, nch_x, s_f, s_b)


def _retention(z, log_gamma, cos_t, sin_t, gn_w, layer, B, n_lat, n_ctx):
    W = N_RET_HEADS * LANES
    H = N_RET_HEADS
    ctx0 = (B * n_lat) // n_ctx

    def xspec(col):
        return pl.BlockSpec((n_lat, LANES), lambda b, h: (b, col * H + h))

    def cspec(col):
        return pl.BlockSpec((n_ctx, LANES), lambda b, h: (ctx0 + b, col * H + h))

    nch_x = n_lat // RET_CHUNK
    nch_c = n_ctx // RET_CHUNK
    nch = max(nch_x, nch_c)
    return pl.pallas_call(
        functools.partial(_retention_body, layer=layer, nch_x=nch_x, nch_c=nch_c, k_scale=float(LANES) ** -0.5),
        out_shape=(jax.ShapeDtypeStruct((B * n_lat, W), BF16), jax.ShapeDtypeStruct((B * n_ctx, W), BF16)),
        grid=(B, H),
        in_specs=[
            pl.BlockSpec(memory_space=pltpu.SMEM),
            xspec(0), xspec(1), xspec(2), xspec(3),
            cspec(0), cspec(1), cspec(2), cspec(3),
            pl.BlockSpec((n_lat, LANES), lambda b, h: (0, 0)),
            pl.BlockSpec((n_lat, LANES), lambda b, h: (0, 0)),
            pl.BlockSpec((1, 1, LANES), lambda b, h: (layer, 0, h)),
        ],
        out_specs=(pl.BlockSpec((n_lat, LANES), lambda b, h: (b, h)),
                   pl.BlockSpec((n_ctx, LANES), lambda b, h: (b, h))),
        scratch_shapes=[pltpu.VMEM((n_lat, LANES), BF16), pltpu.VMEM((n_lat, LANES), BF16),
                        pltpu.VMEM((nch, 2 * RET_CHUNK, RET_CHUNK), F32),
                        pltpu.VMEM((nch, 2 * RET_CHUNK, RET_CHUNK), BF16)],
        compiler_params=_cparams(("arbitrary", "arbitrary"), 48),
        name="retention",
    )(log_gamma, z, z, z, z, z, z, z, z, cos_t, sin_t, gn_w)


def _branch_body(first_ref, last_ref, u_ref, v_ref, pp_ref, pc_ref, pn_ref, sgn_ref, sgw_ref, sgb_ref,
                 pw_ref, pb_ref, ps_ref, yb_ref, yc_ref):
    C = SG_CHUNK
    i = pl.program_id(0)
    is_first = first_ref[i] == 1
    is_last = last_ref[i] == 1

    u = _gelu_tanh(u_ref[...].astype(F32))
    v = _gelu_tanh(v_ref[...].astype(F32))
    mu = jnp.mean(v, axis=-1, keepdims=True)
    d = v - mu
    var = jnp.mean(d * d, axis=-1, keepdims=True)
    vn = (d * lax.rsqrt(var + EPS) * sgn_ref[0]).astype(BF16)
    for g in range(SG_GROUPS):
        cols = slice(g * LANES, (g + 1) * LANES)
        mixed = _dot(sgw_ref[0, g], vn[:, cols]) + sgb_ref[0, :, cols]
        yb_ref[:, cols] = (u[:, cols] * mixed).astype(yb_ref.dtype)

    r = lax.broadcasted_iota(I32, (C, C), 0)
    s = lax.broadcasted_iota(I32, (C, C), 1)
    rr = lax.broadcasted_iota(I32, (C, 1), 0)
    gw = pc_ref.shape[1] // len(POOL_WINDOWS)
    for gi, w in enumerate(POOL_WINDOWS):
        hw = w // 2
        cols = slice(gi * gw, (gi + 1) * gw)
        a_mid = jnp.where((s >= r - hw) & (s < r + hw), 1.0, 0.0)
        a_prev = jnp.where(is_first, 0.0, jnp.where(s >= r - hw + C, 1.0, 0.0))
        a_next = jnp.where(is_last, 0.0, jnp.where(s < r + hw - C, 1.0, 0.0))
        lo = jnp.where(is_first, jnp.maximum(rr - hw, 0), rr - hw)
        hi = jnp.where(is_last, jnp.minimum(rr + hw, C), rr + hw)
        cnt = (hi - lo).astype(F32)
        pg = pc_ref[:, cols]
        tot = (_dot(a_mid.astype(BF16), pg) + _dot(a_prev.astype(BF16), pp_ref[:, cols])
               + _dot(a_next.astype(BF16), pn_ref[:, cols]))
        pooled = tot / cnt - pg.astype(F32)
        yc = (_dot(pooled.astype(BF16), pw_ref[0, gi]) + pb_ref[0, :, cols]) * ps_ref[0, :, cols]
        yc_ref[:, cols] = yc.astype(yc_ref.dtype)


def _branches_bc(z, first_flag, last_flag, sg_norm_w, sg_w, sgb_full, pool_w, pool_b, pool_scale, layer):
    T = z.shape[0]
    W = N_RET_HEADS * LANES
    C = SG_CHUNK
    nblk = T // C
    row = lambda col: pl.BlockSpec((C, W), lambda i, f, l: (i, col))
    vec = pl.BlockSpec((1, 1, W), lambda i, f, l: (layer, 0, 0))
    grid_spec = pltpu.PrefetchScalarGridSpec(
        num_scalar_prefetch=2,
        grid=(nblk,),
        in_specs=[
            row(4), row(5),
            pl.BlockSpec((C, W), lambda i, f, l: (jnp.maximum(i - 1, 0), 6)),
            row(6),
            pl.BlockSpec((C, W), lambda i, f, l: (jnp.minimum(i + 1, nblk - 1), 6)),
            vec,
            pl.BlockSpec((1, SG_GROUPS, C, C), lambda i, f, l: (layer, 0, 0, 0)),
            pl.BlockSpec((1, C, W), lambda i, f, l: (layer, 0, 0)),
            pl.BlockSpec((1,) + pool_w.shape[1:], lambda i, f, l: (layer, 0, 0, 0)),
            vec, vec,
        ],
        out_specs=(pl.BlockSpec((C, W), lambda i, f, l: (i, 0)), pl.BlockSpec((C, W), lambda i, f, l: (i, 0))),
    )
    return pl.pallas_call(
        _branch_body,
        out_shape=(jax.ShapeDtypeStruct((T, W), BF16), jax.ShapeDtypeStruct((T, W), BF16)),
        grid_spec=grid_spec,
        compiler_params=_cparams(("arbitrary",), 32),
        name="branches_bc",
    )(first_flag, last_flag, z, z, z, z, z, sg_norm_w, sg_w, sgb_full, pool_w, pool_b, pool_scale)


def _merge_body(r_ref, yax_ref, yac_ref, yb_ref, yc_ref, gu_ref, gb_ref, wbr_ref, o_ref, gub_ref, wbb_ref,
                *, n_lat_tiles):
    i = pl.program_id(1)

    @pl.when(i == 0)
    def _():
        gub_ref[...] = gu_ref[0].astype(BF16)
        wbb_ref[...] = wbr_ref[0].astype(BF16)

    r = r_ref[...]
    ya = jnp.where(i < n_lat_tiles, yax_ref[...], yac_ref[...])
    acc = None
    for k, y in enumerate((ya, yb_ref[...], yc_ref[...])):
        gate = _sigmoid(_dot(r, gub_ref[k]) + gb_ref[0, k])
        term = gate * _dot(y, wbb_ref[k])
        acc = term if acc is None else acc + term
    o_ref[...] = acc.astype(o_ref.dtype)


def _merge(r, ya_x, ya_c, yb, yc, gate_up, gate_b, w_branch, layer, tm, tn):
    T, R = r.shape
    W = yb.shape[1]
    D = gate_up.shape[3]
    n_lat_tiles = ya_x.shape[0] // tm
    yspec = pl.BlockSpec((tm, W), lambda j, i: (i, 0))
    return pl.pallas_call(
        functools.partial(_merge_body, n_lat_tiles=n_lat_tiles),
        out_shape=jax.ShapeDtypeStruct((T, D), BF16),
        grid=(D // tn, T // tm),
        in_specs=[
            pl.BlockSpec((tm, R), lambda j, i: (i, 0)),
            pl.BlockSpec((tm, W), lambda j, i: (jnp.minimum(i, n_lat_tiles - 1), 0)),
            pl.BlockSpec((tm, W), lambda j, i: (jnp.maximum(i - n_lat_tiles, 0), 0)),
            yspec, yspec,
            pl.BlockSpec((1, 3, R, tn), lambda j, i: (layer, 0, 0, j)),
            pl.BlockSpec((1, 3, 1, tn), lambda j, i: (layer, 0, 0, j)),
            pl.BlockSpec((1, 3, W, tn), lambda j, i: (layer, 0, 0, j)),
        ],
        out_specs=pl.BlockSpec((tm, tn), lambda j, i: (i, j)),
        scratch_shapes=[pltpu.VMEM((3, R, tn), BF16), pltpu.VMEM((3, W, tn), BF16)],
        compiler_params=_cparams(("arbitrary", "arbitrary"), 56),
        name="merge",
    )(r, ya_x, ya_c, yb, yc, gate_up, gate_b, w_branch)


def _norm_router_body(h_ref, w_ref, mod_ref, rw_ref, rb_ref, n2p_ref, meta_ref, wts_ref, cnt_ref, run_ref,
                      *, layer, shift_idx, scale_idx):
    i = pl.program_id(0)

    @pl.when(i == 0)
    def _():
        run_ref[...] = jnp.zeros_like(run_ref)

    n2 = _rms_modulate(h_ref[...], w_ref[0], mod_ref, layer, shift_idx, scale_idx)
    n2p_ref[...] = _pack_bf16_pairs(n2)

    logits = _dot(n2.astype(BF16), rw_ref[0]) + rb_ref[0]
    tr = logits.shape[0]
    lane = lax.broadcasted_iota(I32, logits.shape, 1)
    work = jnp.where(lane < N_EXPERTS, logits, NEG_BIG)
    vals, hots, idxs = [], [], []
    for _ in range(TOP_K):
        m = jnp.max(work, axis=-1, keepdims=True)
        idx = jnp.min(jnp.where(work == m, lane, LANES), axis=-1, keepdims=True)
        hot = lane == idx
        vals.append(m)
        hots.append(hot)
        idxs.append(idx)
        work = jnp.where(hot, NEG_BIG, work)
    es = [jnp.exp(v - vals[0]) for v in vals]
    inv = 1.0 / (es[0] + es[1] + es[2] + es[3])

    hot_all = jnp.zeros(logits.shape, F32)
    for hot in hots:
        hot_all = hot_all + jnp.where(hot, 1.0, 0.0)
    rr = lax.broadcasted_iota(I32, (tr, tr), 0)
    cc = lax.broadcasted_iota(I32, (tr, tr), 1)
    earlier = jnp.where(rr > cc, 1.0, 0.0).astype(BF16)
    base = _dot(earlier, hot_all.astype(BF16)) + run_ref[...]

    meta = jnp.zeros(logits.shape, I32)
    wts = jnp.zeros(logits.shape, F32)
    for k in range(TOP_K):
        rank = jnp.sum(jnp.where(hots[k], base, 0.0), axis=-1, keepdims=True).astype(I32)
        meta = jnp.where(lane == k, idxs[k], meta)
        meta = jnp.where(lane == TOP_K + k, rank, meta)
        wts = jnp.where(lane == k, es[k] * inv, wts)
    meta_ref[...] = meta
    wts_ref[...] = wts
    run_ref[...] = run_ref[...] + jnp.sum(hot_all, axis=0, keepdims=True)
    cnt_ref[...] = jnp.broadcast_to(run_ref[...], cnt_ref.shape)


def _norm_router(h, norm_w, mods, router_wp, router_bp, layer, group_of_tile, tm, shift_idx, scale_idx):
    T, D = h.shape
    return pl.pallas_call(
        functools.partial(_norm_router_body, layer=layer, shift_idx=shift_idx, scale_idx=scale_idx),
        out_shape=(jax.ShapeDtypeStruct((T, D // 2), U32), jax.ShapeDtypeStruct((T, LANES), I32),
                   jax.ShapeDtypeStruct((T, LANES), F32), jax.ShapeDtypeStruct((COND_ROWS, LANES), F32)),
        grid=(T // tm,),
        in_specs=[
            pl.BlockSpec((tm, D), lambda i: (i, 0)),
            pl.BlockSpec((1, 1, D), lambda i: (layer, 0, 0)),
            pl.BlockSpec((1, 1, N_MOD, D), lambda i: (layer, group_of_tile(i), 0, 0)),
            pl.BlockSpec((1, D, LANES), lambda i: (layer, 0, 0)),
            pl.BlockSpec((1, 1, LANES), lambda i: (layer, 0, 0)),
        ],
        out_specs=(pl.BlockSpec((tm, D // 2), lambda i: (i, 0)), pl.BlockSpec((tm, LANES), lambda i: (i, 0)),
                   pl.BlockSpec((tm, LANES), lambda i: (i, 0)), pl.BlockSpec((COND_ROWS, LANES), lambda i: (0, 0))),
        scratch_shapes=[pltpu.VMEM((1, LANES), F32)],
        compiler_params=_cparams(("arbitrary",), 48),
        name="norm_router",
    )(h, norm_w, mods, router_wp, router_bp)


def _row_token_body(dest_ref, po_ref, cnt_ref, inv_ref, *, n_pairs, n_rows):
    def pad_expert(e, carry):
        def f(p, c):
            inv_ref[p] = 0
            return c
        lax.fori_loop(po_ref[e] + cnt_ref[e], po_ref[e + 1], f, 0)
        return carry

    lax.fori_loop(0, N_EXPERTS, pad_expert, 0)

    def tail(p, c):
        inv_ref[p] = 0
        return c

    lax.fori_loop(po_ref[N_EXPERTS], n_rows, tail, 0)

    def scatter(a, c):
        inv_ref[dest_ref[a]] = lax.shift_right_logical(a, TOP_K.bit_length() - 1)
        return c

    lax.fori_loop(0, n_pairs, scatter, 0, unroll=8)


def _row_token(dest, po, cnt, n_rows):
    smem = pl.BlockSpec(memory_space=pltpu.SMEM)
    return pl.pallas_call(
        functools.partial(_row_token_body, n_pairs=dest.shape[0], n_rows=n_rows),
        out_shape=jax.ShapeDtypeStruct((n_rows,), I32),
        in_specs=[smem, smem, smem],
        out_specs=smem,
        name="row_token",
    )(dest, po, cnt)


def _expert_body(inv_ref, texp_ref, nused_ref, x_hbm, w1_ref, b1_ref, w2_ref, b2_ref, perm_ref, y_ref,
                 xbuf0, xbuf1, w1b, w2b, sem, *, tile, d_expert):
    i = pl.program_id(0)
    n_used = nused_ref[0]
    bufs = (xbuf0, xbuf1)

    def row_copy(t, r, s):
        tok = inv_ref[t * tile + r]
        return pltpu.make_async_copy(x_hbm.at[pl.ds(tok, 1), :], bufs[s].at[pl.ds(r, 1), :], sem.at[s])

    def start_tile(t, s):
        for r in range(tile):
            row_copy(t, r, s).start()

    def wait_tile(t, s):
        def f(r, c):
            row_copy(t, r, s).wait()
            return c
        lax.fori_loop(0, tile, f, 0, unroll=8)

    @pl.when(i == 0)
    def _():
        w2b[...] = jnp.zeros_like(w2b)
        start_tile(0, 0)

    def step(s):
        changed = jnp.logical_or(i == 0, texp_ref[i] != texp_ref[jnp.maximum(i - 1, 0)])

        @pl.when(changed)
        def _():
            w1b[...] = w1_ref[0, 0].astype(BF16)
            w2b[0:d_expert, :] = w2_ref[0, 0].astype(BF16)

        wait_tile(i, s)
        nxt = jnp.minimum(i + 1, n_used - 1)
        start_tile(nxt, 1 - s)
        lo, hi = _unpack_bf16_pairs(bufs[s][...])
        half = lo.shape[1]
        gu = (_dot(lo.astype(BF16), w1b[0:half, :]) + _dot(hi.astype(BF16), w1b[half:2 * half, :])
              + b1_ref[0, 0])
        gp = _dot(gu.astype(BF16), perm_ref[...])
        glu = jnp.minimum(gp[:, :MXU_DIM], SWIGLU_LIMIT)
        lin = jnp.clip(gp[:, MXU_DIM:], -SWIGLU_LIMIT, SWIGLU_LIMIT)
        act = glu * _sigmoid(SWIGLU_ALPHA * glu) * (lin + 1.0)
        y = _dot(act.astype(BF16), w2b[...]) + b2_ref[0, 0]
        y_ref[...] = _pack_bf16_pairs(y)

        @pl.when(i == n_used - 1)
        def _():
            wait_tile(nxt, 1 - s)

    for s in range(2):
        pl.when(jnp.logical_and(i < n_used, lax.rem(i, 2) == s))(functools.partial(step, s))

    @pl.when(i >= n_used)
    def _():
        y_ref[...] = jnp.zeros_like(y_ref)


def _experts(n2p, inv, texp, n_used, w1, b1, w2, b2, perm, layer, n_rows, tile):
    half = n2p.shape[1]
    E, D, F2 = w1.shape[1:]
    F = F2 // 2
    n_tiles = n_rows // tile
    wsel = lambda i, inv, te, nu: (layer, te[i], 0, 0)
    grid_spec = pltpu.PrefetchScalarGridSpec(
        num_scalar_prefetch=3,
        grid=(n_tiles,),
        in_specs=[
            pl.BlockSpec(memory_space=pl.ANY),
            pl.BlockSpec((1, 1, D, F2), wsel),
            pl.BlockSpec((1, 1, 1, F2), wsel),
            pl.BlockSpec((1, 1, F, D), wsel),
            pl.BlockSpec((1, 1, 1, D), wsel),
            pl.BlockSpec((F2, 2 * MXU_DIM), lambda i, inv, te, nu: (0, 0)),
        ],
        out_specs=pl.BlockSpec((tile, half), lambda i, inv, te, nu: (i, 0)),
        scratch_shapes=[pltpu.VMEM((tile, half), U32), pltpu.VMEM((tile, half), U32), pltpu.VMEM((D, F2), BF16),
                        pltpu.VMEM((MXU_DIM, D), BF16), pltpu.SemaphoreType.DMA((2,))],
    )
    return pl.pallas_call(
        functools.partial(_expert_body, tile=tile, d_expert=F),
        out_shape=jax.ShapeDtypeStruct((n_rows, half), U32),
        grid_spec=grid_spec,
        compiler_params=_cparams(("arbitrary",), 56),
        name="experts",
    )(inv, texp, n_used, n2p, w1, b1, w2, b2, perm)


def _combine_body(dest_ref, y_hbm, wts_ref, h_ref, mod_ref, o_ref, ybuf0, ybuf1, sem, *, tile, n_tiles, gate_idx):
    i = pl.program_id(0)
    bufs = (ybuf0, ybuf1)
    n_groups = tile // SUBLANES
    half = ybuf0.shape[3]
    chunk = min(half, 8 * LANES)

    def group_copies(t, g, s):
        out = []
        for j in range(SUBLANES):
            for k in range(TOP_K):
                p = dest_ref[(t * tile + g * SUBLANES + j) * TOP_K + k]
                out.append(pltpu.make_async_copy(y_hbm.at[pl.ds(p, 1), :], bufs[s].at[k, g, pl.ds(j, 1), :],
                                                 sem.at[s]))
        return out

    @pl.when(i == 0)
    def _():
        def f(g, c):
            for cp in group_copies(0, g, 0):
                cp.start()
            return c
        lax.fori_loop(0, n_groups, f, 0)

    def step(s, prefetch):
        def wait_group(g, c):
            for cp in group_copies(i, g, s):
                cp.wait()
            return c

        lax.fori_loop(0, n_groups, wait_group, 0)

        def mix_group(g, c):
            if prefetch:
                for cp in group_copies(i + 1, g, 1 - s):
                    cp.start()
            rows = pl.ds(pl.multiple_of(g * SUBLANES, SUBLANES), SUBLANES)
            w_blk = wts_ref[rows, :]
            ws = [w_blk[:, k:k + 1] for k in range(TOP_K)]
            for c0 in range(0, half, chunk):
                acc_lo = None
                acc_hi = None
                for k in range(TOP_K):
                    lo, hi = _unpack_bf16_pairs(bufs[s][k, g, :, c0:c0 + chunk])
                    acc_lo = ws[k] * lo if acc_lo is None else acc_lo + ws[k] * lo
                    acc_hi = ws[k] * hi if acc_hi is None else acc_hi + ws[k] * hi
                for base, acc in ((c0, acc_lo), (half + c0, acc_hi)):
                    gate = mod_ref[0, 0, pl.ds(gate_idx, 1), base:base + chunk]
                    o_ref[rows, base:base + chunk] = h_ref[rows, base:base + chunk] + gate * acc
            return c

        lax.fori_loop(0, n_groups, mix_group, 0)

    last = n_tiles - 1
    for s in range(2):
        pl.when(jnp.logical_and(i < last, lax.rem(i, 2) == s))(functools.partial(step, s, True))
    pl.when(i == last)(functools.partial(step, last % 2, False))


def _combine(ysp, dest, wts, h, mods, layer, group_of_tile, tile, gate_idx):
    T, D = h.shape
    half = D // 2
    n_tiles = T // tile
    grid_spec = pltpu.PrefetchScalarGridSpec(
        num_scalar_prefetch=1,
        grid=(n_tiles,),
        in_specs=[
            pl.BlockSpec(memory_space=pl.ANY),
            pl.BlockSpec((tile, LANES), lambda i, d: (i, 0)),
            pl.BlockSpec((tile, D), lambda i, d: (i, 0)),
            pl.BlockSpec((1, 1, N_MOD, D), lambda i, d: (layer, group_of_tile(i), 0, 0)),
        ],
        out_specs=pl.BlockSpec((tile, D), lambda i, d: (i, 0)),
        scratch_shapes=[pltpu.VMEM((TOP_K, tile // SUBLANES, SUBLANES, half), U32),
                        pltpu.VMEM((TOP_K, tile // SUBLANES, SUBLANES, half), U32),
                        pltpu.SemaphoreType.DMA((2,))],
    )
    return pl.pallas_call(
        functools.partial(_combine_body, tile=tile, n_tiles=n_tiles, gate_idx=gate_idx),
        out_shape=jax.ShapeDtypeStruct((T, D), F32),
        grid_spec=grid_spec,
        compiler_params=_cparams(("arbitrary",), 56),
        name="combine",
    )(dest, ysp, wts, h, mods)


def _deinterleave_matrix(f2):
    p = np.zeros((f2, 2 * MXU_DIM), np.float32)
    j = np.arange(f2 // 2)
    p[2 * j, j] = 1.0
    p[2 * j + 1, MXU_DIM + j] = 1.0
    return jnp.asarray(p, BF16)


@jax.jit
def _forward(x, c, ctx, c_ctx, ada_w, ada_b, norm1_w, norm2_w, w_in, ret_decay, ret_gn_w,
             sg_norm_w, sg_w, sg_b, pool_w, pool_b, pool_scale, w_branch, gate_down, gate_up,
             gate_b, w_out, router_w, router_b, w1, b1, w2, b2, final_norm_w):
    B, n_lat, D = x.shape
    n_ctx = ctx.shape[1]
    depth = ada_w.shape[0]
    W = N_RET_HEADS * LANES
    E, _, F2 = w1.shape[1:]
    F = F2 // 2
    T = B * (n_lat + n_ctx)
    assert B + 1 <= COND_ROWS and n_lat % GRID_W == 0 and E == N_EXPERTS and F <= MXU_DIM
    assert n_lat % RET_CHUNK == 0 and n_ctx % RET_CHUNK == 0 and (B * n_lat) % n_ctx == 0

    tg = _pick(math.gcd(n_lat, B * n_ctx), (512, 256, 128))
    assert B * n_ctx == tg, "the merge kernel takes the context rows as one tile"
    group_of = lambda rows: (lambda i: jnp.minimum((i * rows) // n_lat, B))
    tm = _pick(T, (1088, 1024, 768, 512, 384, 256, 128))
    tc = _pick(tg, (COMBINE_TILE, 128))

    h = jnp.concatenate([x.reshape(B * n_lat, D), ctx.reshape(B * n_ctx, D)], axis=0)
    cond = jnp.zeros((COND_ROWS, D), F32).at[:B].set(c).at[B].set(c_ctx)
    mods = _adaln(cond, ada_w, ada_b).reshape(depth, COND_ROWS, N_MOD, D)

    t = jnp.arange(n_lat)
    n_freq = LANES // 4
    freqs = ROPE_BASE ** (-jnp.arange(n_freq, dtype=F32) / n_freq)
    ang_r = (t // GRID_W).astype(F32)[:, None] * freqs
    ang_c = (t % GRID_W).astype(F32)[:, None] * freqs
    cos_t = jnp.concatenate([jnp.cos(ang_r)] * 2 + [jnp.cos(ang_c)] * 2, axis=-1)
    sin_t = jnp.concatenate([-jnp.sin(ang_r), jnp.sin(ang_r), -jnp.sin(ang_c), jnp.sin(ang_c)], axis=-1)

    blk = jnp.arange(T // SG_CHUNK)
    lat_blk, ctx_blk = n_lat // SG_CHUNK, n_ctx // SG_CHUNK
    in_lat = blk < B * lat_blk
    local = jnp.where(in_lat, blk % lat_blk, (blk - B * lat_blk) % ctx_blk)
    seq_blk = jnp.where(in_lat, lat_blk, ctx_blk)
    first_flag = (local == 0).astype(I32)
    last_flag = (local == seq_blk - 1).astype(I32)

    log_gamma = jax.nn.log_sigmoid(ret_decay.astype(F32))
    vec = lambda a: a.reshape(depth, 1, a.shape[-1])
    norm1_v, norm2_v, gn_v, sgn_v, pb_v, ps_v = map(vec, (norm1_w, norm2_w, ret_gn_w, sg_norm_w, pool_b, pool_scale))
    sgb_full = jnp.repeat(jnp.swapaxes(sg_b, 1, 2), LANES, axis=2)
    sg_wb = sg_w.astype(BF16)
    pool_wb = pool_w.astype(BF16)
    gate_b4 = gate_b.reshape(depth, 3, 1, D)
    router_wp = jnp.zeros((depth, D, LANES), BF16).at[:, :, :E].set(router_w.astype(BF16))
    router_bp = jnp.zeros((depth, 1, LANES), F32).at[:, 0, :E].set(router_b)
    b1_4 = b1.reshape(depth, E, 1, F2)
    b2_4 = b2.reshape(depth, E, 1, D)
    perm = _deinterleave_matrix(F2)

    n_pairs = T * TOP_K
    n_rows = -(-(n_pairs + E * (EXPERT_TILE - 1)) // EXPERT_TILE) * EXPERT_TILE
    tile_start = jnp.arange(n_rows // EXPERT_TILE, dtype=I32) * EXPERT_TILE
    expert_ids = jnp.arange(E, dtype=I32)

    for l in range(depth):
        n1 = _norm_mod(h, norm1_v, mods, l, group_of(tg), tg, 0, 1)
        z = _matmul(n1, w_in, l, tm, 512)
        r = _matmul(n1, gate_down, l, tm, gate_down.shape[2])
        ya_x, ya_c = _retention(z, log_gamma, cos_t, sin_t, gn_v, l, B, n_lat, n_ctx)
        yb, yc = _branches_bc(z, first_flag, last_flag, sgn_v, sg_wb, sgb_full, pool_wb, pb_v, ps_v, l)
        m = _merge(r, ya_x, ya_c, yb, yc, gate_up, gate_b4, w_branch, l, tg, 512)
        h = _matmul_residual(m, w_out, h, mods, l, group_of(tg), tg, 512, 2)

        n2p, meta, wts, counts = _norm_router(h, norm2_v, mods, router_wp, router_bp, l, group_of(tg), tg, 3, 4)
        cnt = counts[0, :E].astype(I32)
        padded = ((cnt + EXPERT_TILE - 1) // EXPERT_TILE) * EXPERT_TILE
        po = jnp.concatenate([jnp.zeros((1,), I32), jnp.cumsum(padded, dtype=I32)])
        texp = jnp.minimum(jnp.sum(tile_start[:, None] >= po[None, 1:], axis=1, dtype=I32), E - 1)
        n_used = (po[E] // EXPERT_TILE).reshape(1)
        ids = meta[:, :TOP_K]
        start_of = jnp.sum(jnp.where(ids[:, :, None] == expert_ids, po[:E], 0), axis=-1, dtype=I32)
        dest = (start_of + meta[:, TOP_K:2 * TOP_K]).reshape(n_pairs)
        inv = _row_token(dest, po, cnt, n_rows)
        ysp = _experts(n2p, inv, texp, n_used, w1, b1_4, w2, b2_4, perm, l, n_rows, EXPERT_TILE)
        h = _combine(ysp, dest, wts, h, mods, l, group_of(tc), tc, 5)

    out = _final_norm(h, final_norm_w, B * n_lat, tg)
    return out.reshape(B, n_lat, D)


def kernel(x, c, ctx, c_ctx, ada_w, ada_b, norm1_w, norm2_w, w_in, ret_decay, ret_gn_w, sg_norm_w, sg_w, sg_b, pool_w, pool_b, pool_scale, w_branch, gate_down, gate_up, gate_b, w_out, router_w, router_b, w1, b1, w2, b2, final_norm_w):
    return _forward(x, c, ctx, c_ctx, ada_w, ada_b, norm1_w, norm2_w, w_in, ret_decay, ret_gn_w,
                    sg_norm_w, sg_w, sg_b, pool_w, pool_b, pool_scale, w_branch, gate_down, gate_up,
                    gate_b, w_out, router_w, router_b, w1, b1, w2, b2, final_norm_w)
```

```python
import functools
import math

import jax
import jax.numpy as jnp
from jax import lax
from jax.experimental import pallas as pl
from jax.experimental.pallas import tpu as pltpu

F32 = jnp.float32
BF16 = jnp.bfloat16
U32 = jnp.uint32
I32 = jnp.int32

GRID_W = 64
N_RET_HEADS = 8
RET_CHUNK = 128
ROPE_BASE = 10000.0
SG_CHUNK = 128
SG_GROUPS = 8
POOL_WINDOWS = (2, 4, 8, 16)
N_EXPERTS = 32
TOP_K = 4
SWIGLU_LIMIT = 7.0
SWIGLU_ALPHA = 1.702
N_MOD = 6
EPS = 1e-6

LANES = 128
SUBLANES = 8
MXU_DIM = 256
COND_ROWS = 8
NEG_BIG = -1e30
EXPERT_TILE = 256
COMBINE_TILE = 256
MIB = 1024 * 1024


def _cparams(sem, vmem_mib):
    return pltpu.CompilerParams(dimension_semantics=sem, vmem_limit_bytes=int(vmem_mib * MIB))


def _pick(n, cands):
    for c in cands:
        if n % c == 0:
            return c
    raise ValueError(f"no tile in {cands} divides {n}")


def _dot(a, b):
    return jnp.dot(a, b, preferred_element_type=F32)


def _dot_tn(a, b):
    return lax.dot_general(a, b, (((0,), (0,)), ((), ())), preferred_element_type=F32)


def _dot_nt(a, b):
    return lax.dot_general(a, b, (((1,), (1,)), ((), ())), preferred_element_type=F32)


def _sigmoid(x):
    return 1.0 / (1.0 + jnp.exp(-x))


def _gelu_tanh(x):
    return 0.5 * x * (1.0 + jnp.tanh(math.sqrt(2.0 / math.pi) * (x + 0.044715 * (x * x * x))))


def _pack_bf16_pairs(x):
    n = x.shape[1] // 2
    xb = x.astype(BF16).astype(F32)
    lo = lax.shift_right_logical(lax.bitcast_convert_type(xb[:, :n], U32), jnp.uint32(16))
    hi = lax.bitcast_convert_type(xb[:, n:], U32) & jnp.uint32(0xFFFF0000)
    return lo | hi


def _unpack_bf16_pairs(w):
    lo = lax.bitcast_convert_type(lax.shift_left(w, jnp.uint32(16)), F32)
    hi = lax.bitcast_convert_type(w & jnp.uint32(0xFFFF0000), F32)
    return lo, hi


def _adaln_body(cond_ref, w_ref, b_ref, o_ref):
    s = cond_ref[...]
    s = (s * _sigmoid(s)).astype(BF16)
    o_ref[0] = _dot(s, w_ref[0].astype(BF16)) + b_ref[0]


def _adaln(cond, ada_w, ada_b):
    L, D, M = ada_w.shape
    tn = _pick(M, (512, 256, 128))
    return pl.pallas_call(
        _adaln_body,
        out_shape=jax.ShapeDtypeStruct((L, COND_ROWS, M), F32),
        grid=(L, M // tn),
        in_specs=[
            pl.BlockSpec((COND_ROWS, D), lambda l, j: (0, 0)),
            pl.BlockSpec((1, D, tn), lambda l, j: (l, 0, j)),
            pl.BlockSpec((1, 1, tn), lambda l, j: (l, 0, j)),
        ],
        out_specs=pl.BlockSpec((1, COND_ROWS, tn), lambda l, j: (l, 0, j)),
        compiler_params=_cparams(("arbitrary", "arbitrary"), 40),
        name="adaln",
    )(cond, ada_w, ada_b.reshape(L, 1, M))


def _rms_modulate(x, w, mod_ref, layer, shift_idx, scale_idx):
    ms = jnp.mean(x * x, axis=-1, keepdims=True)
    y = x * lax.rsqrt(ms + EPS) * w
    shift = mod_ref[0, 0, pl.ds(shift_idx, 1), :]
    scale = mod_ref[0, 0, pl.ds(scale_idx, 1), :]
    return y * (1.0 + scale) + shift


def _norm_mod_body(h_ref, w_ref, mod_ref, o_ref, *, layer, shift_idx, scale_idx):
    o_ref[...] = _rms_modulate(h_ref[...], w_ref[0], mod_ref, layer, shift_idx, scale_idx).astype(o_ref.dtype)


def _norm_mod(h, w, mods, layer, group_of_tile, tm, shift_idx, scale_idx):
    T, D = h.shape
    return pl.pallas_call(
        functools.partial(_norm_mod_body, layer=layer, shift_idx=shift_idx, scale_idx=scale_idx),
        out_shape=jax.ShapeDtypeStruct((T, D), BF16),
        grid=(T // tm,),
        in_specs=[
            pl.BlockSpec((tm, D), lambda i: (i, 0)),
            pl.BlockSpec((1, 1, D), lambda i: (layer, 0, 0)),
            pl.BlockSpec((1, 1, N_MOD, D), lambda i: (layer, group_of_tile(i), 0, 0)),
        ],
        out_specs=pl.BlockSpec((tm, D), lambda i: (i, 0)),
        compiler_params=_cparams(("arbitrary",), 40),
        name="norm_mod",
    )(h, w, mods)


def _final_norm_body(h_ref, w_ref, o_ref):
    x = h_ref[...]
    ms = jnp.mean(x * x, axis=-1, keepdims=True)
    o_ref[...] = x * lax.rsqrt(ms + EPS) * w_ref[...]


def _final_norm(h, w, rows, tm):
    D = h.shape[1]
    return pl.pallas_call(
        _final_norm_body,
        out_shape=jax.ShapeDtypeStruct((rows, D), F32),
        grid=(rows // tm,),
        in_specs=[pl.BlockSpec((tm, D), lambda i: (i, 0)), pl.BlockSpec((1, D), lambda i: (0, 0))],
        out_specs=pl.BlockSpec((tm, D), lambda i: (i, 0)),
        compiler_params=_cparams(("arbitrary",), 40),
        name="final_norm",
    )(h, w.reshape(1, D))


def _mm_body(x_ref, w_ref, o_ref, wb_ref):
    @pl.when(pl.program_id(1) == 0)
    def _():
        wb_ref[...] = w_ref[0].astype(BF16)

    o_ref[...] = _dot(x_ref[...], wb_ref[...]).astype(o_ref.dtype)


def _matmul(x, w, layer, tm, tn, out_dtype=BF16):
    T, K = x.shape
    N = w.shape[2]
    return pl.pallas_call(
        _mm_body,
        out_shape=jax.ShapeDtypeStruct((T, N), out_dtype),
        grid=(N // tn, T // tm),
        in_specs=[
            pl.BlockSpec((tm, K), lambda j, i: (i, 0)),
            pl.BlockSpec((1, K, tn), lambda j, i: (layer, 0, j)),
        ],
        out_specs=pl.BlockSpec((tm, tn), lambda j, i: (i, j)),
        scratch_shapes=[pltpu.VMEM((K, tn), BF16)],
        compiler_params=_cparams(("arbitrary", "arbitrary"), 56),
        name="matmul",
    )(x, w)


def _mm_res_body(x_ref, w_ref, h_ref, mod_ref, o_ref, wb_ref, *, gate_idx):
    @pl.when(pl.program_id(1) == 0)
    def _():
        wb_ref[...] = w_ref[0].astype(BF16)

    gate = mod_ref[0, 0, pl.ds(gate_idx, 1), :]
    o_ref[...] = h_ref[...] + gate * _dot(x_ref[...], wb_ref[...])


def _matmul_residual(x, w, h, mods, layer, group_of_tile, tm, tn, gate_idx):
    T, K = x.shape
    N = w.shape[2]
    return pl.pallas_call(
        functools.partial(_mm_res_body, gate_idx=gate_idx),
        out_shape=jax.ShapeDtypeStruct((T, N), F32),
        grid=(N // tn, T // tm),
        in_specs=[
            pl.BlockSpec((tm, K), lambda j, i: (i, 0)),
            pl.BlockSpec((1, K, tn), lambda j, i: (layer, 0, j)),
            pl.BlockSpec((tm, tn), lambda j, i: (i, j)),
            pl.BlockSpec((1, 1, N_MOD, tn), lambda j, i: (layer, group_of_tile(i), 0, j)),
        ],
        out_specs=pl.BlockSpec((tm, tn), lambda j, i: (i, j)),
        scratch_shapes=[pltpu.VMEM((K, tn), BF16)],
        compiler_params=_cparams(("arbitrary", "arbitrary"), 56),
        name="matmul_residual",
    )(x, w, h, mods)


def _retention_body(lg_ref, qx_ref, kx_ref, vx_ref, gx_ref, qc_ref, kc_ref, vc_ref, gc_ref,
                    cos_ref, sin_ref, gnw_ref, ox_ref, oc_ref, qr_ref, kr_ref, u_ref, st_ref,
                    *, layer, nch_x, nch_c, k_scale):
    C = RET_CHUNK
    head = pl.program_id(1)
    lgf = lg_ref[layer, 0, head]
    lgb = lg_ref[layer, 1, head]

    ri = lax.broadcasted_iota(I32, (C, C), 0)
    ci = lax.broadcasted_iota(I32, (C, C), 1)
    diff = (ri - ci).astype(F32)
    dmask = (jnp.where(diff >= 0.0, jnp.exp(lgf * jnp.maximum(diff, 0.0)), 0.0)
             + jnp.where(diff <= 0.0, jnp.exp(lgb * jnp.maximum(-diff, 0.0)), 0.0))
    pos = lax.broadcasted_iota(I32, (C, 1), 0).astype(F32)
    q_dec_f = jnp.exp(lgf * (pos + 1.0))
    k_dec_f = jnp.exp(lgf * (C - 1.0 - pos))
    q_dec_b = jnp.exp(lgb * (C - pos))
    k_dec_b = jnp.exp(lgb * pos)
    full_c = jnp.full((C, C), float(C), F32)
    chunk_dec_f = jnp.exp(lgf * full_c)
    chunk_dec_b = jnp.exp(lgb * full_c)
    swap = jnp.where((ri ^ 32) == ci, 1.0, 0.0).astype(BF16)
    gnw = gnw_ref[0]

    def rows_of(c):
        return pl.ds(pl.multiple_of(c * C, C), C)

    def rope_chunk(c, carry):
        rows = rows_of(c)
        cos = cos_ref[rows, :]
        sin = sin_ref[rows, :]
        q = qx_ref[rows, :]
        k = kx_ref[rows, :]
        qr_ref[rows, :] = (q.astype(F32) * cos + _dot(q, swap) * sin).astype(BF16)
        kr_ref[rows, :] = ((k.astype(F32) * cos + _dot(k, swap) * sin) * k_scale).astype(BF16)
        return carry

    lax.fori_loop(0, nch_x, rope_chunk, 0, unroll=min(4, nch_x))

    def segment(q_at, k_at, v_at, g_ref, o_ref, nch, s_f0, s_b0):
        def incr(c, carry):
            kf = k_at(c).astype(F32)
            kd = jnp.concatenate([kf * k_dec_f, kf * k_dec_b], axis=1).astype(BF16)
            u_ref[c] = _dot_tn(kd, v_at(c))
            return carry

        lax.fori_loop(0, nch, incr, 0, unroll=min(4, nch))

        def fwd_scan(c, s):
            st_ref[c, 0:C, :] = s.astype(BF16)
            return chunk_dec_f * s + u_ref[c, 0:C, :]

        s_f_final = lax.fori_loop(0, nch, fwd_scan, s_f0)

        def bwd_scan(t, s):
            c = nch - 1 - t
            st_ref[c, C:2 * C, :] = s.astype(BF16)
            return chunk_dec_b * s + u_ref[c, C:2 * C, :]

        s_b_final = lax.fori_loop(0, nch, bwd_scan, s_b0)

        def emit(c, carry):
            rows = rows_of(c)
            q = q_at(c)
            v = v_at(c)
            qf = q.astype(F32)
            scores = _dot_nt(q, k_at(c)) * dmask
            qd = jnp.concatenate([qf * q_dec_f, qf * q_dec_b], axis=1).astype(BF16)
            y = _dot(scores.astype(BF16), v) + _dot(qd, st_ref[c])
            mu = jnp.mean(y, axis=-1, keepdims=True)
            d = y - mu
            var = jnp.mean(d * d, axis=-1, keepdims=True)
            yn = d * lax.rsqrt(var + EPS) * gnw
            g = g_ref[rows, :].astype(F32)
            o_ref[rows, :] = (g * _sigmoid(g) * yn).astype(o_ref.dtype)
            return carry

        lax.fori_loop(0, nch, emit, 0, unroll=min(4, nch))
        return s_f_final, s_b_final

    def at(ref, scale=None):
        def get(c):
            x = ref[rows_of(c), :]
            if scale is not None:
                x = (x.astype(F32) * scale).astype(BF16)
            return x
        return get

    zero = jnp.zeros((C, C), F32)
    s_f, s_b = segment(at(qc_ref), at(kc_ref, k_scale), at(vc_ref), gc_ref, oc_ref, nch_c, zero, zero)
    segment(at(qr_ref), at(kr_ref), at(vx_ref), gx_ref, ox_ref, nch_x, s_f, s_b)


def _retention(z, log_gamma, cos_t, sin_t, gn_w, layer, B, n_lat, n_ctx):
    W = N_RET_HEADS * LANES
    H = N_RET_HEADS
    ctx0 = (B * n_lat) // n_ctx

    def xspec(col):
        return pl.BlockSpec((n_lat, LANES), lambda b, h: (b, col * H + h))

    def cspec(col):
        return pl.BlockSpec((n_ctx, LANES), lambda b, h: (ctx0 + b, col * H + h))

    nch_x = n_lat // RET_CHUNK
    nch_c = n_ctx // RET_CHUNK
    nch = max(nch_x, nch_c)
    return pl.pallas_call(
        functools.partial(_retention_body, layer=layer, nch_x=nch_x, nch_c=nch_c, k_scale=float(LANES) ** -0.5),
        out_shape=(jax.ShapeDtypeStruct((B * n_lat, W), BF16), jax.ShapeDtypeStruct((B * n_ctx, W), BF16)),
        grid=(B, H),
        in_specs=[
            pl.BlockSpec(memory_space=pltpu.SMEM),
            xspec(0), xspec(1), xspec(2), xspec(3),
            cspec(0), cspec(1), cspec(2), cspec(3),
            pl.BlockSpec((n_lat, LANES), lambda b, h: (0, 0)),
            pl.BlockSpec((n_lat, LANES), lambda b, h: (0, 0)),
            pl.BlockSpec((1, 1, LANES), lambda b, h: (layer, 0, h)),
        ],
        out_specs=(pl.BlockSpec((n_lat, LANES), lambda b, h: (b, h)),
                   pl.BlockSpec((n_ctx, LANES), lambda b, h: (b, h))),
        scratch_shapes=[pltpu.VMEM((n_lat, LANES), BF16), pltpu.VMEM((n_lat, LANES), BF16),
                        pltpu.VMEM((nch, 2 * RET_CHUNK, RET_CHUNK), F32),
                        pltpu.VMEM((nch, 2 * RET_CHUNK, RET_CHUNK), BF16)],
        compiler_params=_cparams(("arbitrary", "arbitrary"), 48),
        name="retention",
    )(log_gamma, z, z, z, z, z, z, z, z, cos_t, sin_t, gn_w)


def _branch_body(first_ref, last_ref, u_ref, v_ref, pp_ref, pc_ref, pn_ref, sgn_ref, sgw_ref, sgb_ref,
                 pw_ref, pb_ref, ps_ref, yb_ref, yc_ref):
    C = SG_CHUNK
    i = pl.program_id(0)
    is_first = first_ref[i] == 1
    is_last = last_ref[i] == 1

    u = _gelu_tanh(u_ref[...].astype(F32))
    v = _gelu_tanh(v_ref[...].astype(F32))
    mu = jnp.mean(v, axis=-1, keepdims=True)
    d = v - mu
    var = jnp.mean(d * d, axis=-1, keepdims=True)
    vn = (d * lax.rsqrt(var + EPS) * sgn_ref[0]).astype(BF16)
    for g in range(SG_GROUPS):
        cols = slice(g * LANES, (g + 1) * LANES)
        mixed = _dot(sgw_ref[0, g], vn[:, cols]) + sgb_ref[0, :, cols]
        yb_ref[:, cols] = (u[:, cols] * mixed).astype(yb_ref.dtype)

    r = lax.broadcasted_iota(I32, (C, C), 0)
    s = lax.broadcasted_iota(I32, (C, C), 1)
    rr = lax.broadcasted_iota(I32, (C, 1), 0)
    gw = pc_ref.shape[1] // len(POOL_WINDOWS)
    for gi, w in enumerate(POOL_WINDOWS):
        hw = w // 2
        cols = slice(gi * gw, (gi + 1) * gw)
        a_mid = jnp.where((s >= r - hw) & (s < r + hw), 1.0, 0.0)
        a_prev = jnp.where(is_first, 0.0, jnp.where(s >= r - hw + C, 1.0, 0.0))
        a_next = jnp.where(is_last, 0.0, jnp.where(s < r + hw - C, 1.0, 0.0))
        lo = jnp.where(is_first, jnp.maximum(rr - hw, 0), rr - hw)
        hi = jnp.where(is_last, jnp.minimum(rr + hw, C), rr + hw)
        cnt = (hi - lo).astype(F32)
        pg = pc_ref[:, cols]
        tot = (_dot(a_mid.astype(BF16), pg) + _dot(a_prev.astype(BF16), pp_ref[:, cols])
               + _dot(a_next.astype(BF16), pn_ref[:, cols]))
        pooled = tot / cnt - pg.astype(F32)
        yc = (_dot(pooled.astype(BF16), pw_ref[0, gi]) + pb_ref[0, :, cols]) * ps_ref[0, :, cols]
        yc_ref[:, cols] = yc.astype(yc_ref.dtype)


def _branches_bc(z, first_flag, last_flag, sg_norm_w, sg_w, sgb_full, pool_w, pool_b, pool_scale, layer):
    T = z.shape[0]
    W = N_RET_HEADS * LANES
    C = SG_CHUNK
    nblk = T // C
    row = lambda col: pl.BlockSpec((C, W), lambda i, f, l: (i, col))
    vec = pl.BlockSpec((1, 1, W), lambda i, f, l: (layer, 0, 0))
    grid_spec = pltpu.PrefetchScalarGridSpec(
        num_scalar_prefetch=2,
        grid=(nblk,),
        in_specs=[
            row(4), row(5),
            pl.BlockSpec((C, W), lambda i, f, l: (jnp.maximum(i - 1, 0), 6)),
            row(6),
            pl.BlockSpec((C, W), lambda i, f, l: (jnp.minimum(i + 1, nblk - 1), 6)),
            vec,
            pl.BlockSpec((1, SG_GROUPS, C, C), lambda i, f, l: (layer, 0, 0, 0)),
            pl.BlockSpec((1, C, W), lambda i, f, l: (layer, 0, 0)),
            pl.BlockSpec((1,) + pool_w.shape[1:], lambda i, f, l: (layer, 0, 0, 0)),
            vec, vec,
        ],
        out_specs=(pl.BlockSpec((C, W), lambda i, f, l: (i, 0)), pl.BlockSpec((C, W), lambda i, f, l: (i, 0))),
    )
    return pl.pallas_call(
        _branch_body,
        out_shape=(jax.ShapeDtypeStruct((T, W), BF16), jax.ShapeDtypeStruct((T, W), BF16)),
        grid_spec=grid_spec,
        compiler_params=_cparams(("arbitrary",), 32),
        name="branches_bc",
    )(first_flag, last_flag, z, z, z, z, z, sg_norm_w, sg_w, sgb_full, pool_w, pool_b, pool_scale)


def _merge_body(r_ref, yax_ref, yac_ref, yb_ref, yc_ref, gu_ref, gb_ref, wbr_ref, o_ref, gub_ref, wbb_ref,
                *, n_lat_tiles):
    i = pl.program_id(1)

    @pl.when(i == 0)
    def _():
        gub_ref[...] = gu_ref[0].astype(BF16)
        wbb_ref[...] = wbr_ref[0].astype(BF16)

    r = r_ref[...]
    ya = jnp.where(i < n_lat_tiles, yax_ref[...], yac_ref[...])
    acc = None
    for k, y in enumerate((ya, yb_ref[...], yc_ref[...])):
        gate = _sigmoid(_dot(r, gub_ref[k]) + gb_ref[0, k])
        term = gate * _dot(y, wbb_ref[k])
        acc = term if acc is None else acc + term
    o_ref[...] = acc.astype(o_ref.dtype)


def _merge(r, ya_x, ya_c, yb, yc, gate_up, gate_b, w_branch, layer, tm, tn):
    T, R = r.shape
    W = yb.shape[1]
    D = gate_up.shape[3]
    n_lat_tiles = ya_x.shape[0] // tm
    yspec = pl.BlockSpec((tm, W), lambda j, i: (i, 0))
    return pl.pallas_call(
        functools.partial(_merge_body, n_lat_tiles=n_lat_tiles),
        out_shape=jax.ShapeDtypeStruct((T, D), BF16),
        grid=(D // tn, T // tm),
        in_specs=[
            pl.BlockSpec((tm, R), lambda j, i: (i, 0)),
            pl.BlockSpec((tm, W), lambda j, i: (jnp.minimum(i, n_lat_tiles - 1), 0)),
            pl.BlockSpec((tm, W), lambda j, i: (jnp.maximum(i - n_lat_tiles, 0), 0)),
            yspec, yspec,
            pl.BlockSpec((1, 3, R, tn), lambda j, i: (layer, 0, 0, j)),
            pl.BlockSpec((1, 3, 1, tn), lambda j, i: (layer, 0, 0, j)),
            pl.BlockSpec((1, 3, W, tn), lambda j, i: (layer, 0, 0, j)),
        ],
        out_specs=pl.BlockSpec((tm, tn), lambda j, i: (i, j)),
        scratch_shapes=[pltpu.VMEM((3, R, tn), BF16), pltpu.VMEM((3, W, tn), BF16)],
        compiler_params=_cparams(("arbitrary", "arbitrary"), 56),
        name="merge",
    )(r, ya_x, ya_c, yb, yc, gate_up, gate_b, w_branch)


def _norm_router_body(h_ref, w_ref, mod_ref, rw_ref, rb_ref, n2p_ref, meta_ref, wts_ref, cnt_ref, run_ref,
                      *, layer, shift_idx, scale_idx):
    i = pl.program_id(0)

    @pl.when(i == 0)
    def _():
        run_ref[...] = jnp.zeros_like(run_ref)

    n2 = _rms_modulate(h_ref[...], w_ref[0], mod_ref, layer, shift_idx, scale_idx)
    n2p_ref[...] = _pack_bf16_pairs(n2)

    logits = _dot(n2.astype(BF16), rw_ref[0]) + rb_ref[0]
    tr = logits.shape[0]
    lane = lax.broadcasted_iota(I32, logits.shape, 1)
    work = jnp.where(lane < N_EXPERTS, logits, NEG_BIG)
    vals, hots, idxs = [], [], []
    for _ in range(TOP_K):
        m = jnp.max(work, axis=-1, keepdims=True)
        idx = jnp.min(jnp.where(work == m, lane, LANES), axis=-1, keepdims=True)
        hot = lane == idx
        vals.append(m)
        hots.append(hot)
        idxs.append(idx)
        work = jnp.where(hot, NEG_BIG, work)
    es = [jnp.exp(v - vals[0]) for v in vals]
    inv = 1.0 / (es[0] + es[1] + es[2] + es[3])

    hot_all = jnp.zeros(logits.shape, F32)
    for hot in hots:
        hot_all = hot_all + jnp.where(hot, 1.0, 0.0)
    rr = lax.broadcasted_iota(I32, (tr, tr), 0)
    cc = lax.broadcasted_iota(I32, (tr, tr), 1)
    earlier = jnp.where(rr > cc, 1.0, 0.0).astype(BF16)
    base = _dot(earlier, hot_all.astype(BF16)) + run_ref[...]

    meta = jnp.zeros(logits.shape, I32)
    wts = jnp.zeros(logits.shape, F32)
    for k in range(TOP_K):
        rank = jnp.sum(jnp.where(hots[k], base, 0.0), axis=-1, keepdims=True).astype(I32)
        meta = jnp.where(lane == k, idxs[k], meta)
        meta = jnp.where(lane == TOP_K + k, rank, meta)
        wts = jnp.where(lane == k, es[k] * inv, wts)
    meta_ref[...] = meta
    wts_ref[...] = wts
    run_ref[...] = run_ref[...] + jnp.sum(hot_all, axis=0, keepdims=True)
    cnt_ref[...] = jnp.broadcast_to(run_ref[...], cnt_ref.shape)


def _norm_router(h, norm_w, mods, router_wp, router_bp, layer, group_of_tile, tm, shift_idx, scale_idx):
    T, D = h.shape
    return pl.pallas_call(
        functools.partial(_norm_router_body, layer=layer, shift_idx=shift_idx, scale_idx=scale_idx),
        out_shape=(jax.ShapeDtypeStruct((T, D // 2), U32), jax.ShapeDtypeStruct((T, LANES), I32),
                   jax.ShapeDtypeStruct((T, LANES), F32), jax.ShapeDtypeStruct((COND_ROWS, LANES), F32)),
        grid=(T // tm,),
        in_specs=[
            pl.BlockSpec((tm, D), lambda i: (i, 0)),
            pl.BlockSpec((1, 1, D), lambda i: (layer, 0, 0)),
            pl.BlockSpec((1, 1, N_MOD, D), lambda i: (layer, group_of_tile(i), 0, 0)),
            pl.BlockSpec((1, D, LANES), lambda i: (layer, 0, 0)),
            pl.BlockSpec((1, 1, LANES), lambda i: (layer, 0, 0)),
        ],
        out_specs=(pl.BlockSpec((tm, D // 2), lambda i: (i, 0)), pl.BlockSpec((tm, LANES), lambda i: (i, 0)),
                   pl.BlockSpec((tm, LANES), lambda i: (i, 0)), pl.BlockSpec((COND_ROWS, LANES), lambda i: (0, 0))),
        scratch_shapes=[pltpu.VMEM((1, LANES), F32)],
        compiler_params=_cparams(("arbitrary",), 48),
        name="norm_router",
    )(h, norm_w, mods, router_wp, router_bp)


def _row_token_body(dest_ref, po_ref, cnt_ref, inv_ref, *, n_pairs, n_rows):
    def pad_expert(e, carry):
        def f(p, c):
            inv_ref[p] = 0
            return c
        lax.fori_loop(po_ref[e] + cnt_ref[e], po_ref[e + 1], f, 0)
        return carry

    lax.fori_loop(0, N_EXPERTS, pad_expert, 0)

    def tail(p, c):
        inv_ref[p] = 0
        return c

    lax.fori_loop(po_ref[N_EXPERTS], n_rows, tail, 0)

    def scatter(a, c):
        inv_ref[dest_ref[a]] = lax.shift_right_logical(a, TOP_K.bit_length() - 1)
        return c

    lax.fori_loop(0, n_pairs, scatter, 0, unroll=8)


def _row_token(dest, po, cnt, n_rows):
    smem = pl.BlockSpec(memory_space=pltpu.SMEM)
    return pl.pallas_call(
        functools.partial(_row_token_body, n_pairs=dest.shape[0], n_rows=n_rows),
        out_shape=jax.ShapeDtypeStruct((n_rows,), I32),
        in_specs=[smem, smem, smem],
        out_specs=smem,
        name="row_token",
    )(dest, po, cnt)


def _expert_body(inv_ref, texp_ref, nused_ref, x_hbm, w1t_ref, b1_ref, w2_ref, b2_ref, y_ref,
                 xbuf0, xbuf1, xbuf2, w1c, w2b, stage, sem, *, tile, d_expert):
    i = pl.program_id(0)
    n_used = nused_ref[0]
    bufs = (xbuf0, xbuf1, xbuf2)
    n_buf = len(bufs)

    def row_copy(t, r, s):
        tok = inv_ref[t * tile + r]
        return pltpu.make_async_copy(x_hbm.at[pl.ds(tok, 1), :], bufs[s].at[pl.ds(r, 1), :], sem.at[s])

    def start_tile(t, s):
        for r in range(tile):
            row_copy(t, r, s).start(priority=r % 2)

    def wait_tile(t, s):
        def f(r, c):
            row_copy(t, r, s).wait()
            return c
        lax.fori_loop(0, tile, f, 0, unroll=8)

    last = n_used - 1

    @pl.when(i == 0)
    def _():
        w1c[...] = jnp.zeros_like(w1c)
        w2b[...] = jnp.zeros_like(w2b)
        start_tile(0, 0)
        start_tile(jnp.minimum(1, last), 1)

    def step(s):
        changed = jnp.logical_or(i == 0, texp_ref[i] != texp_ref[jnp.maximum(i - 1, 0)])

        @pl.when(changed)
        def _():
            for kc in range(stage.shape[0]):
                cols = slice(kc * LANES, (kc + 1) * LANES)
                stage[kc] = w1t_ref[0, 0, :, cols]
                w1c[0:d_expert, cols] = stage[kc, pl.ds(0, d_expert, stride=2), :].astype(BF16)
                w1c[MXU_DIM:MXU_DIM + d_expert, cols] = stage[kc, pl.ds(1, d_expert, stride=2), :].astype(BF16)
            w2b[0:d_expert, :] = w2_ref[0, 0].astype(BF16)

        wait_tile(i, s)
        nxt = jnp.minimum(i + 2, last)
        start_tile(nxt, (s + 2) % n_buf)
        lo, hi = _unpack_bf16_pairs(bufs[s][...])
        half = lo.shape[1]
        gp = (_dot_nt(lo.astype(BF16), w1c[:, 0:half]) + _dot_nt(hi.astype(BF16), w1c[:, half:2 * half])
              + b1_ref[0, 0])
        glu = jnp.minimum(gp[:, :MXU_DIM], SWIGLU_LIMIT)
        lin = jnp.clip(gp[:, MXU_DIM:], -SWIGLU_LIMIT, SWIGLU_LIMIT)
        act = glu * _sigmoid(SWIGLU_ALPHA * glu) * (lin + 1.0)
        y = _dot(act.astype(BF16), w2b[...]) + b2_ref[0, 0]
        y_ref[...] = _pack_bf16_pairs(y)

        @pl.when(i == last)
        def _():
            wait_tile(last, (s + 1) % n_buf)
            wait_tile(last, (s + 2) % n_buf)

    for s in range(n_buf):
        pl.when(jnp.logical_and(i < n_used, lax.rem(i, n_buf) == s))(functools.partial(step, s))

    @pl.when(i >= n_used)
    def _():
        y_ref[...] = jnp.zeros_like(y_ref)


def _experts(n2p, inv, texp, n_used, w1t, b1c, w2, b2, layer, n_rows, tile):
    half = n2p.shape[1]
    E, F2, D = w1t.shape[1:]
    F = F2 // 2
    n_tiles = n_rows // tile
    wsel = lambda i, inv, te, nu: (layer, te[i], 0, 0)
    grid_spec = pltpu.PrefetchScalarGridSpec(
        num_scalar_prefetch=3,
        grid=(n_tiles,),
        in_specs=[
            pl.BlockSpec(memory_space=pl.ANY),
            pl.BlockSpec((1, 1, F2, D), wsel),
            pl.BlockSpec((1, 1, 1, 2 * MXU_DIM), wsel),
            pl.BlockSpec((1, 1, F, D), wsel),
            pl.BlockSpec((1, 1, 1, D), wsel),
        ],
        out_specs=pl.BlockSpec((tile, half), lambda i, inv, te, nu: (i, 0)),
        scratch_shapes=[pltpu.VMEM((tile, half), U32), pltpu.VMEM((tile, half), U32), pltpu.VMEM((tile, half), U32),
                        pltpu.VMEM((2 * MXU_DIM, D), BF16), pltpu.VMEM((MXU_DIM, D), BF16),
                        pltpu.VMEM((D // LANES, F2, LANES), F32), pltpu.SemaphoreType.DMA((3,))],
    )
    return pl.pallas_call(
        functools.partial(_expert_body, tile=tile, d_expert=F),
        out_shape=jax.ShapeDtypeStruct((n_rows, half), U32),
        grid_spec=grid_spec,
        compiler_params=_cparams(("arbitrary",), 56),
        name="experts",
    )(inv, texp, n_used, n2p, w1t, b1c, w2, b2)


def _combine_body(dest_ref, y_hbm, wts_ref, h_ref, mod_ref, o_ref, ybuf0, ybuf1, sem, *, tile, n_tiles, gate_idx):
    i = pl.program_id(0)
    bufs = (ybuf0, ybuf1)
    n_groups = tile // SUBLANES
    half = ybuf0.shape[3]
    chunk = min(half, 8 * LANES)

    def group_copies(t, g, s):
        out = []
        for j in range(SUBLANES):
            for k in range(TOP_K):
                p = dest_ref[(t * tile + g * SUBLANES + j) * TOP_K + k]
                out.append(pltpu.make_async_copy(y_hbm.at[pl.ds(p, 1), :], bufs[s].at[k, g, pl.ds(j, 1), :],
                                                 sem.at[s]))
        return out

    @pl.when(i == 0)
    def _():
        def f(g, c):
            for n, cp in enumerate(group_copies(0, g, 0)):
                cp.start(priority=n % 2)
            return c
        lax.fori_loop(0, n_groups, f, 0)

    def step(s, prefetch):
        def wait_group(g, c):
            for cp in group_copies(i, g, s):
                cp.wait()
            return c

        lax.fori_loop(0, n_groups, wait_group, 0)

        def mix_group(g, c):
            if prefetch:
                for n, cp in enumerate(group_copies(i + 1, g, 1 - s)):
                    cp.start(priority=n % 2)
            rows = pl.ds(pl.multiple_of(g * SUBLANES, SUBLANES), SUBLANES)
            w_blk = wts_ref[rows, :]
            ws = [w_blk[:, k:k + 1] for k in range(TOP_K)]
            for c0 in range(0, half, chunk):
                acc_lo = None
                acc_hi = None
                for k in range(TOP_K):
                    lo, hi = _unpack_bf16_pairs(bufs[s][k, g, :, c0:c0 + chunk])
                    acc_lo = ws[k] * lo if acc_lo is None else acc_lo + ws[k] * lo
                    acc_hi = ws[k] * hi if acc_hi is None else acc_hi + ws[k] * hi
                for base, acc in ((c0, acc_lo), (half + c0, acc_hi)):
                    gate = mod_ref[0, 0, pl.ds(gate_idx, 1), base:base + chunk]
                    o_ref[rows, base:base + chunk] = h_ref[rows, base:base + chunk] + gate * acc
            return c

        lax.fori_loop(0, n_groups, mix_group, 0)

    last = n_tiles - 1
    for s in range(2):
        pl.when(jnp.logical_and(i < last, lax.rem(i, 2) == s))(functools.partial(step, s, True))
    pl.when(i == last)(functools.partial(step, last % 2, False))


def _combine(ysp, dest, wts, h, mods, layer, group_of_tile, tile, gate_idx):
    T, D = h.shape
    half = D // 2
    n_tiles = T // tile
    grid_spec = pltpu.PrefetchScalarGridSpec(
        num_scalar_prefetch=1,
        grid=(n_tiles,),
        in_specs=[
            pl.BlockSpec(memory_space=pl.ANY),
            pl.BlockSpec((tile, LANES), lambda i, d: (i, 0)),
            pl.BlockSpec((tile, D), lambda i, d: (i, 0)),
            pl.BlockSpec((1, 1, N_MOD, D), lambda i, d: (layer, group_of_tile(i), 0, 0)),
        ],
        out_specs=pl.BlockSpec((tile, D), lambda i, d: (i, 0)),
        scratch_shapes=[pltpu.VMEM((TOP_K, tile // SUBLANES, SUBLANES, half), U32),
                        pltpu.VMEM((TOP_K, tile // SUBLANES, SUBLANES, half), U32),
                        pltpu.SemaphoreType.DMA((2,))],
    )
    return pl.pallas_call(
        functools.partial(_combine_body, tile=tile, n_tiles=n_tiles, gate_idx=gate_idx),
        out_shape=jax.ShapeDtypeStruct((T, D), F32),
        grid_spec=grid_spec,
        compiler_params=_cparams(("arbitrary",), 56),
        name="combine",
    )(dest, ysp, wts, h, mods)


@jax.jit
def _forward(x, c, ctx, c_ctx, ada_w, ada_b, norm1_w, norm2_w, w_in, ret_decay, ret_gn_w,
             sg_norm_w, sg_w, sg_b, pool_w, pool_b, pool_scale, w_branch, gate_down, gate_up,
             gate_b, w_out, router_w, router_b, w1, b1, w2, b2, final_norm_w):
    B, n_lat, D = x.shape
    n_ctx = ctx.shape[1]
    depth = ada_w.shape[0]
    W = N_RET_HEADS * LANES
    E, _, F2 = w1.shape[1:]
    F = F2 // 2
    T = B * (n_lat + n_ctx)
    assert B + 1 <= COND_ROWS and n_lat % GRID_W == 0 and E == N_EXPERTS and F <= MXU_DIM
    assert n_lat % RET_CHUNK == 0 and n_ctx % RET_CHUNK == 0 and (B * n_lat) % n_ctx == 0

    tg = _pick(math.gcd(n_lat, B * n_ctx), (512, 256, 128))
    assert B * n_ctx == tg, "the merge kernel takes the context rows as one tile"
    group_of = lambda rows: (lambda i: jnp.minimum((i * rows) // n_lat, B))
    tm = _pick(T, (1088, 1024, 768, 512, 384, 256, 128))
    tc = _pick(tg, (COMBINE_TILE, 128))

    h = jnp.concatenate([x.reshape(B * n_lat, D), ctx.reshape(B * n_ctx, D)], axis=0)
    cond = jnp.zeros((COND_ROWS, D), F32).at[:B].set(c).at[B].set(c_ctx)
    mods = _adaln(cond, ada_w, ada_b).reshape(depth, COND_ROWS, N_MOD, D)

    t = jnp.arange(n_lat)
    n_freq = LANES // 4
    freqs = ROPE_BASE ** (-jnp.arange(n_freq, dtype=F32) / n_freq)
    ang_r = (t // GRID_W).astype(F32)[:, None] * freqs
    ang_c = (t % GRID_W).astype(F32)[:, None] * freqs
    cos_t = jnp.concatenate([jnp.cos(ang_r)] * 2 + [jnp.cos(ang_c)] * 2, axis=-1)
    sin_t = jnp.concatenate([-jnp.sin(ang_r), jnp.sin(ang_r), -jnp.sin(ang_c), jnp.sin(ang_c)], axis=-1)

    blk = jnp.arange(T // SG_CHUNK)
    lat_blk, ctx_blk = n_lat // SG_CHUNK, n_ctx // SG_CHUNK
    in_lat = blk < B * lat_blk
    local = jnp.where(in_lat, blk % lat_blk, (blk - B * lat_blk) % ctx_blk)
    seq_blk = jnp.where(in_lat, lat_blk, ctx_blk)
    first_flag = (local == 0).astype(I32)
    last_flag = (local == seq_blk - 1).astype(I32)

    log_gamma = jax.nn.log_sigmoid(ret_decay.astype(F32))
    vec = lambda a: a.reshape(depth, 1, a.shape[-1])
    norm1_v, norm2_v, gn_v, sgn_v, pb_v, ps_v = map(vec, (norm1_w, norm2_w, ret_gn_w, sg_norm_w, pool_b, pool_scale))
    sgb_full = jnp.repeat(jnp.swapaxes(sg_b, 1, 2), LANES, axis=2)
    sg_wb = sg_w.astype(BF16)
    pool_wb = pool_w.astype(BF16)
    gate_b4 = gate_b.reshape(depth, 3, 1, D)
    router_wp = jnp.zeros((depth, D, LANES), BF16).at[:, :, :E].set(router_w.astype(BF16))
    router_bp = jnp.zeros((depth, 1, LANES), F32).at[:, 0, :E].set(router_b)
    w1t = jnp.swapaxes(w1, 2, 3)
    b1c = (jnp.zeros((depth, E, 1, 2 * MXU_DIM), F32).at[:, :, 0, :F].set(b1[:, :, 0::2])
           .at[:, :, 0, MXU_DIM:MXU_DIM + F].set(b1[:, :, 1::2]))
    b2_4 = b2.reshape(depth, E, 1, D)

    n_pairs = T * TOP_K
    n_rows = -(-(n_pairs + E * (EXPERT_TILE - 1)) // EXPERT_TILE) * EXPERT_TILE
    tile_start = jnp.arange(n_rows // EXPERT_TILE, dtype=I32) * EXPERT_TILE
    expert_ids = jnp.arange(E, dtype=I32)

    for l in range(depth):
        n1 = _norm_mod(h, norm1_v, mods, l, group_of(tg), tg, 0, 1)
        z = _matmul(n1, w_in, l, tm, 512)
        r = _matmul(n1, gate_down, l, tm, gate_down.shape[2])
        ya_x, ya_c = _retention(z, log_gamma, cos_t, sin_t, gn_v, l, B, n_lat, n_ctx)
        yb, yc = _branches_bc(z, first_flag, last_flag, sgn_v, sg_wb, sgb_full, pool_wb, pb_v, ps_v, l)
        m = _merge(r, ya_x, ya_c, yb, yc, gate_up, gate_b4, w_branch, l, tg, 512)
        h = _matmul_residual(m, w_out, h, mods, l, group_of(tg), tg, 512, 2)

        n2p, meta, wts, counts = _norm_router(h, norm2_v, mods, router_wp, router_bp, l, group_of(tg), tg, 3, 4)
        cnt = counts[0, :E].astype(I32)
        padded = ((cnt + EXPERT_TILE - 1) // EXPERT_TILE) * EXPERT_TILE
        po = jnp.concatenate([jnp.zeros((1,), I32), jnp.cumsum(padded, dtype=I32)])
        texp = jnp.minimum(jnp.sum(tile_start[:, None] >= po[None, 1:], axis=1, dtype=I32), E - 1)
        n_used = (po[E] // EXPERT_TILE).reshape(1)
        ids = meta[:, :TOP_K]
        start_of = jnp.sum(jnp.where(ids[:, :, None] == expert_ids, po[:E], 0), axis=-1, dtype=I32)
        dest = (start_of + meta[:, TOP_K:2 * TOP_K]).reshape(n_pairs)
        inv = _row_token(dest, po, cnt, n_rows)
        ysp = _experts(n2p, inv, texp, n_used, w1t, b1c, w2, b2_4, l, n_rows, EXPERT_TILE)
        h = _combine(ysp, dest, wts, h, mods, l, group_of(tc), tc, 5)

    out = _final_norm(h, final_norm_w, B * n_lat, tg)
    return out.reshape(B, n_lat, D)


def kernel(x, c, ctx, c_ctx, ada_w, ada_b, norm1_w, norm2_w, w_in, ret_decay, ret_gn_w, sg_norm_w, sg_w, sg_b, pool_w, pool_b, pool_scale, w_branch, gate_down, gate_up, gate_b, w_out, router_w, router_b, w1, b1, w2, b2, final_norm_w):
    return _forward(x, c, ctx, c_ctx, ada_w, ada_b, norm1_w, norm2_w, w_in, ret_decay, ret_gn_w,
                    sg_norm_w, sg_w, sg_b, pool_w, pool_b, pool_scale, w_branch, gate_down, gate_up,
                    gate_b, w_out, router_w, router_b, w1, b1, w2, b2, final_norm_w)
```

```python
import functools
import math

import jax
import jax.numpy as jnp
from jax import lax
from jax.experimental import pallas as pl
from jax.experimental.pallas import tpu as pltpu

F32 = jnp.float32
BF16 = jnp.bfloat16
U32 = jnp.uint32
I32 = jnp.int32

GRID_W = 64
N_RET_HEADS = 8
RET_CHUNK = 128
ROPE_BASE = 10000.0
SG_CHUNK = 128
SG_GROUPS = 8
POOL_WINDOWS = (2, 4, 8, 16)
N_EXPERTS = 32
TOP_K = 4
SWIGLU_LIMIT = 7.0
SWIGLU_ALPHA = 1.702
N_MOD = 6
EPS = 1e-6

LANES = 128
SUBLANES = 8
MXU_DIM = 256
COND_ROWS = 8
NEG_BIG = -1e30
EXPERT_TILE = 256
COMBINE_TILE = 256
MIB = 1024 * 1024


def _cparams(sem, vmem_mib):
    return pltpu.CompilerParams(dimension_semantics=sem, vmem_limit_bytes=int(vmem_mib * MIB))


def _pick(n, cands):
    for c in cands:
        if n % c == 0:
            return c
    raise ValueError(f"no tile in {cands} divides {n}")


def _dot(a, b):
    return jnp.dot(a, b, preferred_element_type=F32)


def _dot_tn(a, b):
    return lax.dot_general(a, b, (((0,), (0,)), ((), ())), preferred_element_type=F32)


def _dot_nt(a, b):
    return lax.dot_general(a, b, (((1,), (1,)), ((), ())), preferred_element_type=F32)


def _sigmoid(x):
    return 1.0 / (1.0 + jnp.exp(-x))


def _gelu_tanh(x):
    return 0.5 * x * (1.0 + jnp.tanh(math.sqrt(2.0 / math.pi) * (x + 0.044715 * (x * x * x))))


def _pack_bf16_pairs(x):
    n = x.shape[1] // 2
    xb = x.astype(BF16).astype(F32)
    lo = lax.shift_right_logical(lax.bitcast_convert_type(xb[:, :n], U32), jnp.uint32(16))
    hi = lax.bitcast_convert_type(xb[:, n:], U32) & jnp.uint32(0xFFFF0000)
    return lo | hi


def _unpack_bf16_pairs(w):
    lo = lax.bitcast_convert_type(lax.shift_left(w, jnp.uint32(16)), F32)
    hi = lax.bitcast_convert_type(w & jnp.uint32(0xFFFF0000), F32)
    return lo, hi


def _store_slabs(ref, words):
    rows = words.shape[0]
    n_chunks = words.shape[1] // LANES
    for c in range(n_chunks):
        ref[pl.ds(c, rows, stride=n_chunks), :] = words[:, c * LANES:(c + 1) * LANES]


def _load_slabs(ref, rows, n_chunks):
    return jnp.concatenate([ref[pl.ds(c, rows, stride=n_chunks), :] for c in range(n_chunks)], axis=1)


def _adaln_body(cond_ref, w_ref, b_ref, o_ref):
    s = cond_ref[...]
    s = (s * _sigmoid(s)).astype(BF16)
    o_ref[0] = _dot(s, w_ref[0].astype(BF16)) + b_ref[0]


def _adaln(cond, ada_w, ada_b):
    L, D, M = ada_w.shape
    tn = _pick(M, (512, 256, 128))
    return pl.pallas_call(
        _adaln_body,
        out_shape=jax.ShapeDtypeStruct((L, COND_ROWS, M), F32),
        grid=(L, M // tn),
        in_specs=[
            pl.BlockSpec((COND_ROWS, D), lambda l, j: (0, 0)),
            pl.BlockSpec((1, D, tn), lambda l, j: (l, 0, j)),
            pl.BlockSpec((1, 1, tn), lambda l, j: (l, 0, j)),
        ],
        out_specs=pl.BlockSpec((1, COND_ROWS, tn), lambda l, j: (l, 0, j)),
        compiler_params=_cparams(("arbitrary", "arbitrary"), 40),
        name="adaln",
    )(cond, ada_w, ada_b.reshape(L, 1, M))


def _rms_modulate(x, w, mod_ref, layer, shift_idx, scale_idx):
    ms = jnp.mean(x * x, axis=-1, keepdims=True)
    y = x * lax.rsqrt(ms + EPS) * w
    shift = mod_ref[0, 0, pl.ds(shift_idx, 1), :]
    scale = mod_ref[0, 0, pl.ds(scale_idx, 1), :]
    return y * (1.0 + scale) + shift


def _norm_mod_body(h_ref, w_ref, mod_ref, o_ref, *, layer, shift_idx, scale_idx):
    o_ref[...] = _rms_modulate(h_ref[...], w_ref[0], mod_ref, layer, shift_idx, scale_idx).astype(o_ref.dtype)


def _norm_mod(h, w, mods, layer, group_of_tile, tm, shift_idx, scale_idx):
    T, D = h.shape
    return pl.pallas_call(
        functools.partial(_norm_mod_body, layer=layer, shift_idx=shift_idx, scale_idx=scale_idx),
        out_shape=jax.ShapeDtypeStruct((T, D), BF16),
        grid=(T // tm,),
        in_specs=[
            pl.BlockSpec((tm, D), lambda i: (i, 0)),
            pl.BlockSpec((1, 1, D), lambda i: (layer, 0, 0)),
            pl.BlockSpec((1, 1, N_MOD, D), lambda i: (layer, group_of_tile(i), 0, 0)),
        ],
        out_specs=pl.BlockSpec((tm, D), lambda i: (i, 0)),
        compiler_params=_cparams(("arbitrary",), 40),
        name="norm_mod",
    )(h, w, mods)


def _final_norm_body(h_ref, w_ref, o_ref):
    x = h_ref[...]
    ms = jnp.mean(x * x, axis=-1, keepdims=True)
    o_ref[...] = x * lax.rsqrt(ms + EPS) * w_ref[...]


def _final_norm(h, w, rows, tm):
    D = h.shape[1]
    return pl.pallas_call(
        _final_norm_body,
        out_shape=jax.ShapeDtypeStruct((rows, D), F32),
        grid=(rows // tm,),
        in_specs=[pl.BlockSpec((tm, D), lambda i: (i, 0)), pl.BlockSpec((1, D), lambda i: (0, 0))],
        out_specs=pl.BlockSpec((tm, D), lambda i: (i, 0)),
        compiler_params=_cparams(("arbitrary",), 40),
        name="final_norm",
    )(h, w.reshape(1, D))


def _mm_body(x_ref, w_ref, o_ref, wb_ref):
    @pl.when(pl.program_id(1) == 0)
    def _():
        wb_ref[...] = w_ref[0].astype(BF16)

    o_ref[...] = _dot(x_ref[...], wb_ref[...]).astype(o_ref.dtype)


def _matmul(x, w, layer, tm, tn, out_dtype=BF16):
    T, K = x.shape
    N = w.shape[2]
    return pl.pallas_call(
        _mm_body,
        out_shape=jax.ShapeDtypeStruct((T, N), out_dtype),
        grid=(N // tn, T // tm),
        in_specs=[
            pl.BlockSpec((tm, K), lambda j, i: (i, 0)),
            pl.BlockSpec((1, K, tn), lambda j, i: (layer, 0, j)),
        ],
        out_specs=pl.BlockSpec((tm, tn), lambda j, i: (i, j)),
        scratch_shapes=[pltpu.VMEM((K, tn), BF16)],
        compiler_params=_cparams(("arbitrary", "arbitrary"), 56),
        name="matmul",
    )(x, w)


def _mm_res_body(x_ref, w_ref, h_ref, mod_ref, o_ref, wb_ref, *, gate_idx, n_lat, n_groups):
    @pl.when(pl.program_id(1) == 0)
    def _():
        wb_ref[...] = w_ref[0].astype(BF16)

    tm = o_ref.shape[0]
    row = pl.program_id(1) * tm + lax.broadcasted_iota(I32, (tm, 1), 0)
    gate = mod_ref[0, n_groups - 1, pl.ds(gate_idx, 1), :]
    for b in reversed(range(n_groups - 1)):
        gate = jnp.where(row < (b + 1) * n_lat, mod_ref[0, b, pl.ds(gate_idx, 1), :], gate)
    o_ref[...] = h_ref[...] + gate * _dot(x_ref[...], wb_ref[...])


def _matmul_residual(x, w, h, mods, layer, n_lat, n_groups, tm, tn, gate_idx):
    T, K = x.shape
    N = w.shape[2]
    return pl.pallas_call(
        functools.partial(_mm_res_body, gate_idx=gate_idx, n_lat=n_lat, n_groups=n_groups),
        out_shape=jax.ShapeDtypeStruct((T, N), F32),
        grid=(N // tn, T // tm),
        in_specs=[
            pl.BlockSpec((tm, K), lambda j, i: (i, 0)),
            pl.BlockSpec((1, K, tn), lambda j, i: (layer, 0, j)),
            pl.BlockSpec((tm, tn), lambda j, i: (i, j)),
            pl.BlockSpec((1, COND_ROWS, N_MOD, tn), lambda j, i: (layer, 0, 0, j)),
        ],
        out_specs=pl.BlockSpec((tm, tn), lambda j, i: (i, j)),
        scratch_shapes=[pltpu.VMEM((K, tn), BF16)],
        compiler_params=_cparams(("arbitrary", "arbitrary"), 56),
        name="matmul_residual",
    )(x, w, h, mods)


def _retention_body(lg_ref, qx_ref, kx_ref, vx_ref, gx_ref, qc_ref, kc_ref, vc_ref, gc_ref,
                    cos_ref, sin_ref, gnw_ref, ox_ref, oc_ref, qr_ref, kr_ref, u_ref, st_ref,
                    *, layer, nch_x, nch_c, k_scale):
    C = RET_CHUNK
    head = pl.program_id(1)
    lgf = lg_ref[layer, 0, head]
    lgb = lg_ref[layer, 1, head]

    ri = lax.broadcasted_iota(I32, (C, C), 0)
    ci = lax.broadcasted_iota(I32, (C, C), 1)
    diff = (ri - ci).astype(F32)
    dmask = (jnp.where(diff >= 0.0, jnp.exp(lgf * jnp.maximum(diff, 0.0)), 0.0)
             + jnp.where(diff <= 0.0, jnp.exp(lgb * jnp.maximum(-diff, 0.0)), 0.0))
    pos = lax.broadcasted_iota(I32, (C, 1), 0).astype(F32)
    q_dec_f = jnp.exp(lgf * (pos + 1.0))
    k_dec_f = jnp.exp(lgf * (C - 1.0 - pos))
    q_dec_b = jnp.exp(lgb * (C - pos))
    k_dec_b = jnp.exp(lgb * pos)
    full_c = jnp.full((C, C), float(C), F32)
    chunk_dec_f = jnp.exp(lgf * full_c)
    chunk_dec_b = jnp.exp(lgb * full_c)
    swap = jnp.where((ri ^ 32) == ci, 1.0, 0.0).astype(BF16)
    gnw = gnw_ref[0]

    def rows_of(c):
        return pl.ds(pl.multiple_of(c * C, C), C)

    def rope_chunk(c, carry):
        rows = rows_of(c)
        cos = cos_ref[rows, :]
        sin = sin_ref[rows, :]
        q = qx_ref[rows, :]
        k = kx_ref[rows, :]
        qr_ref[rows, :] = (q.astype(F32) * cos + _dot(q, swap) * sin).astype(BF16)
        kr_ref[rows, :] = ((k.astype(F32) * cos + _dot(k, swap) * sin) * k_scale).astype(BF16)
        return carry

    lax.fori_loop(0, nch_x, rope_chunk, 0, unroll=min(4, nch_x))

    def segment(q_at, k_at, v_at, g_ref, o_ref, nch, s_f0, s_b0):
        def incr(c, carry):
            kf = k_at(c).astype(F32)
            kd = jnp.concatenate([kf * k_dec_f, kf * k_dec_b], axis=1).astype(BF16)
            u_ref[c] = _dot_tn(kd, v_at(c))
            return carry

        lax.fori_loop(0, nch, incr, 0, unroll=min(4, nch))

        def fwd_scan(c, s):
            st_ref[c, 0:C, :] = s.astype(BF16)
            return chunk_dec_f * s + u_ref[c, 0:C, :]

        s_f_final = lax.fori_loop(0, nch, fwd_scan, s_f0)

        def bwd_scan(t, s):
            c = nch - 1 - t
            st_ref[c, C:2 * C, :] = s.astype(BF16)
            return chunk_dec_b * s + u_ref[c, C:2 * C, :]

        s_b_final = lax.fori_loop(0, nch, bwd_scan, s_b0)

        def emit(c, carry):
            rows = rows_of(c)
            q = q_at(c)
            v = v_at(c)
            qf = q.astype(F32)
            scores = _dot_nt(q, k_at(c)) * dmask
            qd = jnp.concatenate([qf * q_dec_f, qf * q_dec_b], axis=1).astype(BF16)
            y = _dot(scores.astype(BF16), v) + _dot(qd, st_ref[c])
            mu = jnp.mean(y, axis=-1, keepdims=True)
            d = y - mu
            var = jnp.mean(d * d, axis=-1, keepdims=True)
            yn = d * lax.rsqrt(var + EPS) * gnw
            g = g_ref[rows, :].astype(F32)
            o_ref[rows, :] = (g * _sigmoid(g) * yn).astype(o_ref.dtype)
            return carry

        lax.fori_loop(0, nch, emit, 0, unroll=min(4, nch))
        return s_f_final, s_b_final

    def at(ref, scale=None):
        def get(c):
            x = ref[rows_of(c), :]
            if scale is not None:
                x = (x.astype(F32) * scale).astype(BF16)
            return x
        return get

    zero = jnp.zeros((C, C), F32)
    s_f, s_b = segment(at(qc_ref), at(kc_ref, k_scale), at(vc_ref), gc_ref, oc_ref, nch_c, zero, zero)
    segment(at(qr_ref), at(kr_ref), at(vx_ref), gx_ref, ox_ref, nch_x, s_f, s_b)


def _retention(z, log_gamma, cos_t, sin_t, gn_w, layer, B, n_lat, n_ctx):
    W = N_RET_HEADS * LANES
    H = N_RET_HEADS
    ctx0 = (B * n_lat) // n_ctx

    def xspec(col):
        return pl.BlockSpec((n_lat, LANES), lambda b, h: (b, col * H + h))

    def cspec(col):
        return pl.BlockSpec((n_ctx, LANES), lambda b, h: (ctx0 + b, col * H + h))

    nch_x = n_lat // RET_CHUNK
    nch_c = n_ctx // RET_CHUNK
    nch = max(nch_x, nch_c)
    return pl.pallas_call(
        functools.partial(_retention_body, layer=layer, nch_x=nch_x, nch_c=nch_c, k_scale=float(LANES) ** -0.5),
        out_shape=(jax.ShapeDtypeStruct((B * n_lat, W), BF16), jax.ShapeDtypeStruct((B * n_ctx, W), BF16)),
        grid=(B, H),
        in_specs=[
            pl.BlockSpec(memory_space=pltpu.SMEM),
            xspec(0), xspec(1), xspec(2), xspec(3),
            cspec(0), cspec(1), cspec(2), cspec(3),
            pl.BlockSpec((n_lat, LANES), lambda b, h: (0, 0)),
            pl.BlockSpec((n_lat, LANES), lambda b, h: (0, 0)),
            pl.BlockSpec((1, 1, LANES), lambda b, h: (layer, 0, h)),
        ],
        out_specs=(pl.BlockSpec((n_lat, LANES), lambda b, h: (b, h)),
                   pl.BlockSpec((n_ctx, LANES), lambda b, h: (b, h))),
        scratch_shapes=[pltpu.VMEM((n_lat, LANES), BF16), pltpu.VMEM((n_lat, LANES), BF16),
                        pltpu.VMEM((nch, 2 * RET_CHUNK, RET_CHUNK), F32),
                        pltpu.VMEM((nch, 2 * RET_CHUNK, RET_CHUNK), BF16)],
        compiler_params=_cparams(("arbitrary", "arbitrary"), 48),
        name="retention",
    )(log_gamma, z, z, z, z, z, z, z, z, cos_t, sin_t, gn_w)


def _branch_body(first_ref, last_ref, u_ref, v_ref, pp_ref, pc_ref, pn_ref, sgn_ref, sgw_ref, sgb_ref,
                 pw_ref, pb_ref, ps_ref, yb_ref, yc_ref):
    C = SG_CHUNK
    i = pl.program_id(0)
    is_first = first_ref[i] == 1
    is_last = last_ref[i] == 1

    u = _gelu_tanh(u_ref[...].astype(F32))
    v = _gelu_tanh(v_ref[...].astype(F32))
    mu = jnp.mean(v, axis=-1, keepdims=True)
    d = v - mu
    var = jnp.mean(d * d, axis=-1, keepdims=True)
    vn = (d * lax.rsqrt(var + EPS) * sgn_ref[0]).astype(BF16)
    for g in range(SG_GROUPS):
        cols = slice(g * LANES, (g + 1) * LANES)
        mixed = _dot(sgw_ref[0, g], vn[:, cols]) + sgb_ref[0, :, cols]
        yb_ref[:, cols] = (u[:, cols] * mixed).astype(yb_ref.dtype)

    r = lax.broadcasted_iota(I32, (C, C), 0)
    s = lax.broadcasted_iota(I32, (C, C), 1)
    rr = lax.broadcasted_iota(I32, (C, 1), 0)
    gw = pc_ref.shape[1] // len(POOL_WINDOWS)
    for gi, w in enumerate(POOL_WINDOWS):
        hw = w // 2
        cols = slice(gi * gw, (gi + 1) * gw)
        a_mid = jnp.where((s >= r - hw) & (s < r + hw), 1.0, 0.0)
        a_prev = jnp.where(is_first, 0.0, jnp.where(s >= r - hw + C, 1.0, 0.0))
        a_next = jnp.where(is_last, 0.0, jnp.where(s < r + hw - C, 1.0, 0.0))
        lo = jnp.where(is_first, jnp.maximum(rr - hw, 0), rr - hw)
        hi = jnp.where(is_last, jnp.minimum(rr + hw, C), rr + hw)
        cnt = (hi - lo).astype(F32)
        pg = pc_ref[:, cols]
        tot = (_dot(a_mid.astype(BF16), pg) + _dot(a_prev.astype(BF16), pp_ref[:, cols])
               + _dot(a_next.astype(BF16), pn_ref[:, cols]))
        pooled = tot / cnt - pg.astype(F32)
        yc = (_dot(pooled.astype(BF16), pw_ref[0, gi]) + pb_ref[0, :, cols]) * ps_ref[0, :, cols]
        yc_ref[:, cols] = yc.astype(yc_ref.dtype)


def _branches_bc(z, first_flag, last_flag, sg_norm_w, sg_w, sgb_full, pool_w, pool_b, pool_scale, layer):
    T = z.shape[0]
    W = N_RET_HEADS * LANES
    C = SG_CHUNK
    nblk = T // C
    row = lambda col: pl.BlockSpec((C, W), lambda i, f, l: (i, col))
    vec = pl.BlockSpec((1, 1, W), lambda i, f, l: (layer, 0, 0))
    grid_spec = pltpu.PrefetchScalarGridSpec(
        num_scalar_prefetch=2,
        grid=(nblk,),
        in_specs=[
            row(4), row(5),
            pl.BlockSpec((C, W), lambda i, f, l: (jnp.maximum(i - 1, 0), 6)),
            row(6),
            pl.BlockSpec((C, W), lambda i, f, l: (jnp.minimum(i + 1, nblk - 1), 6)),
            vec,
            pl.BlockSpec((1, SG_GROUPS, C, C), lambda i, f, l: (layer, 0, 0, 0)),
            pl.BlockSpec((1, C, W), lambda i, f, l: (layer, 0, 0)),
            pl.BlockSpec((1,) + pool_w.shape[1:], lambda i, f, l: (layer, 0, 0, 0)),
            vec, vec,
        ],
        out_specs=(pl.BlockSpec((C, W), lambda i, f, l: (i, 0)), pl.BlockSpec((C, W), lambda i, f, l: (i, 0))),
    )
    return pl.pallas_call(
        _branch_body,
        out_shape=(jax.ShapeDtypeStruct((T, W), BF16), jax.ShapeDtypeStruct((T, W), BF16)),
        grid_spec=grid_spec,
        compiler_params=_cparams(("arbitrary",), 32),
        name="branches_bc",
    )(first_flag, last_flag, z, z, z, z, z, sg_norm_w, sg_w, sgb_full, pool_w, pool_b, pool_scale)


def _merge_body(r_ref, ya_ref, yb_ref, yc_ref, gu_ref, gb_ref, wbr_ref, o_ref, gub_ref, wbb_ref):
    @pl.when(pl.program_id(1) == 0)
    def _():
        gub_ref[...] = gu_ref[0].astype(BF16)
        wbb_ref[...] = wbr_ref[0].astype(BF16)

    r = r_ref[...]
    acc = None
    for k, y_ref in enumerate((ya_ref, yb_ref, yc_ref)):
        gate = _sigmoid(_dot(r, gub_ref[k]) + gb_ref[0, k])
        term = gate * _dot(y_ref[...], wbb_ref[k])
        acc = term if acc is None else acc + term
    o_ref[...] = acc.astype(o_ref.dtype)


def _merge(r, ya, yb, yc, gate_up, gate_b, w_branch, layer, tm, tn):
    T, R = r.shape
    W = yb.shape[1]
    D = gate_up.shape[3]
    yspec = pl.BlockSpec((tm, W), lambda j, i: (i, 0))
    return pl.pallas_call(
        _merge_body,
        out_shape=jax.ShapeDtypeStruct((T, D), BF16),
        grid=(D // tn, T // tm),
        in_specs=[
            pl.BlockSpec((tm, R), lambda j, i: (i, 0)),
            yspec, yspec, yspec,
            pl.BlockSpec((1, 3, R, tn), lambda j, i: (layer, 0, 0, j)),
            pl.BlockSpec((1, 3, 1, tn), lambda j, i: (layer, 0, 0, j)),
            pl.BlockSpec((1, 3, W, tn), lambda j, i: (layer, 0, 0, j)),
        ],
        out_specs=pl.BlockSpec((tm, tn), lambda j, i: (i, j)),
        scratch_shapes=[pltpu.VMEM((3, R, tn), BF16), pltpu.VMEM((3, W, tn), BF16)],
        compiler_params=_cparams(("arbitrary", "arbitrary"), 56),
        name="merge",
    )(r, ya, yb, yc, gate_up, gate_b, w_branch)


def _norm_router_body(h_ref, w_ref, mod_ref, rw_ref, rb_ref, n2p_ref, meta_ref, wts_ref, cnt_ref, run_ref,
                      *, layer, shift_idx, scale_idx):
    i = pl.program_id(0)

    @pl.when(i == 0)
    def _():
        run_ref[...] = jnp.zeros_like(run_ref)

    n2 = _rms_modulate(h_ref[...], w_ref[0], mod_ref, layer, shift_idx, scale_idx)
    _store_slabs(n2p_ref, _pack_bf16_pairs(n2))

    logits = _dot(n2.astype(BF16), rw_ref[0]) + rb_ref[0]
    tr = logits.shape[0]
    lane = lax.broadcasted_iota(I32, logits.shape, 1)
    work = jnp.where(lane < N_EXPERTS, logits, NEG_BIG)
    vals, hots, idxs = [], [], []
    for _ in range(TOP_K):
        m = jnp.max(work, axis=-1, keepdims=True)
        idx = jnp.min(jnp.where(work == m, lane, LANES), axis=-1, keepdims=True)
        hot = lane == idx
        vals.append(m)
        hots.append(hot)
        idxs.append(idx)
        work = jnp.where(hot, NEG_BIG, work)
    es = [jnp.exp(v - vals[0]) for v in vals]
    inv = 1.0 / (es[0] + es[1] + es[2] + es[3])

    hot_all = jnp.zeros(logits.shape, F32)
    for hot in hots:
        hot_all = hot_all + jnp.where(hot, 1.0, 0.0)
    rr = lax.broadcasted_iota(I32, (tr, tr), 0)
    cc = lax.broadcasted_iota(I32, (tr, tr), 1)
    earlier = jnp.where(rr > cc, 1.0, 0.0).astype(BF16)
    base = _dot(earlier, hot_all.astype(BF16)) + run_ref[...]

    meta = jnp.zeros(logits.shape, I32)
    wts = jnp.zeros(logits.shape, F32)
    for k in range(TOP_K):
        rank = jnp.sum(jnp.where(hots[k], base, 0.0), axis=-1, keepdims=True).astype(I32)
        meta = jnp.where(lane == k, idxs[k], meta)
        meta = jnp.where(lane == TOP_K + k, rank, meta)
        wts = jnp.where(lane == k, es[k] * inv, wts)
    meta_ref[...] = meta
    wts_ref[...] = wts
    run_ref[...] = run_ref[...] + jnp.sum(hot_all, axis=0, keepdims=True)
    cnt_ref[...] = jnp.broadcast_to(run_ref[...], cnt_ref.shape)


def _norm_router(h, norm_w, mods, router_wp, router_bp, layer, group_of_tile, tm, shift_idx, scale_idx):
    T, D = h.shape
    return pl.pallas_call(
        functools.partial(_norm_router_body, layer=layer, shift_idx=shift_idx, scale_idx=scale_idx),
        out_shape=(jax.ShapeDtypeStruct((T * (D // 2 // LANES), LANES), U32), jax.ShapeDtypeStruct((T, LANES), I32),
                   jax.ShapeDtypeStruct((T, LANES), F32), jax.ShapeDtypeStruct((COND_ROWS, LANES), F32)),
        grid=(T // tm,),
        in_specs=[
            pl.BlockSpec((tm, D), lambda i: (i, 0)),
            pl.BlockSpec((1, 1, D), lambda i: (layer, 0, 0)),
            pl.BlockSpec((1, 1, N_MOD, D), lambda i: (layer, group_of_tile(i), 0, 0)),
            pl.BlockSpec((1, D, LANES), lambda i: (layer, 0, 0)),
            pl.BlockSpec((1, 1, LANES), lambda i: (layer, 0, 0)),
        ],
        out_specs=(pl.BlockSpec((tm * (D // 2 // LANES), LANES), lambda i: (i, 0)),
                   pl.BlockSpec((tm, LANES), lambda i: (i, 0)),
                   pl.BlockSpec((tm, LANES), lambda i: (i, 0)), pl.BlockSpec((COND_ROWS, LANES), lambda i: (0, 0))),
        scratch_shapes=[pltpu.VMEM((1, LANES), F32)],
        compiler_params=_cparams(("arbitrary",), 48),
        name="norm_router",
    )(h, norm_w, mods, router_wp, router_bp)


def _row_token_body(dest_ref, po_ref, cnt_ref, inv_ref, *, n_pairs, n_rows):
    def pad_expert(e, carry):
        def f(p, c):
            inv_ref[p] = 0
            return c
        lax.fori_loop(po_ref[e] + cnt_ref[e], po_ref[e + 1], f, 0)
        return carry

    lax.fori_loop(0, N_EXPERTS, pad_expert, 0)

    def tail(p, c):
        inv_ref[p] = 0
        return c

    lax.fori_loop(po_ref[N_EXPERTS], n_rows, tail, 0)

    def scatter(a, c):
        inv_ref[dest_ref[a]] = lax.shift_right_logical(a, TOP_K.bit_length() - 1)
        return c

    lax.fori_loop(0, n_pairs, scatter, 0, unroll=8)


def _row_token(dest, po, cnt, n_rows):
    smem = pl.BlockSpec(memory_space=pltpu.SMEM)
    return pl.pallas_call(
        functools.partial(_row_token_body, n_pairs=dest.shape[0], n_rows=n_rows),
        out_shape=jax.ShapeDtypeStruct((n_rows,), I32),
        in_specs=[smem, smem, smem],
        out_specs=smem,
        name="row_token",
    )(dest, po, cnt)


def _expert_body(inv_ref, texp_ref, nused_ref, x_hbm, w1t_ref, b1_ref, w2_ref, b2_ref, y_ref,
                 xbuf0, xbuf1, xbuf2, w1c, w2b, stage, sem, *, tile, d_expert):
    i = pl.program_id(0)
    n_used = nused_ref[0]
    bufs = (xbuf0, xbuf1, xbuf2)
    n_buf = len(bufs)

    n_chunks = xbuf0.shape[0] // tile

    def row_copy(t, r, s):
        tok = inv_ref[t * tile + r]
        src = x_hbm.at[pl.ds(pl.multiple_of(tok * n_chunks, n_chunks), n_chunks), :]
        dst = bufs[s].at[pl.ds(pl.multiple_of(r * n_chunks, n_chunks), n_chunks), :]
        return pltpu.make_async_copy(src, dst, sem.at[s])

    def start_tile(t, s):
        for r in range(tile):
            row_copy(t, r, s).start(priority=r % 2)

    def wait_tile(t, s):
        def f(r, c):
            row_copy(t, r, s).wait()
            return c
        lax.fori_loop(0, tile, f, 0, unroll=8)

    last = n_used - 1

    @pl.when(i == 0)
    def _():
        w1c[...] = jnp.zeros_like(w1c)
        w2b[...] = jnp.zeros_like(w2b)
        start_tile(0, 0)
        start_tile(jnp.minimum(1, last), 1)

    def step(s):
        changed = jnp.logical_or(i == 0, texp_ref[i] != texp_ref[jnp.maximum(i - 1, 0)])

        @pl.when(changed)
        def _():
            for kc in range(stage.shape[0]):
                cols = slice(kc * LANES, (kc + 1) * LANES)
                stage[kc] = w1t_ref[0, 0, :, cols]
                w1c[0:d_expert, cols] = stage[kc, pl.ds(0, d_expert, stride=2), :].astype(BF16)
                w1c[MXU_DIM:MXU_DIM + d_expert, cols] = stage[kc, pl.ds(1, d_expert, stride=2), :].astype(BF16)
            w2b[0:d_expert, :] = w2_ref[0, 0].astype(BF16)

        wait_tile(i, s)
        nxt = jnp.minimum(i + 2, last)
        start_tile(nxt, (s + 2) % n_buf)
        lo, hi = _unpack_bf16_pairs(_load_slabs(bufs[s], tile, n_chunks))
        half = lo.shape[1]
        gp = (_dot_nt(lo.astype(BF16), w1c[:, 0:half]) + _dot_nt(hi.astype(BF16), w1c[:, half:2 * half])
              + b1_ref[0, 0])
        glu = jnp.minimum(gp[:, :MXU_DIM], SWIGLU_LIMIT)
        lin = jnp.clip(gp[:, MXU_DIM:], -SWIGLU_LIMIT, SWIGLU_LIMIT)
        act = glu * _sigmoid(SWIGLU_ALPHA * glu) * (lin + 1.0)
        y = _dot(act.astype(BF16), w2b[...]) + b2_ref[0, 0]
        _store_slabs(y_ref, _pack_bf16_pairs(y))

        @pl.when(i == last)
        def _():
            wait_tile(last, (s + 1) % n_buf)
            wait_tile(last, (s + 2) % n_buf)

    for s in range(n_buf):
        pl.when(jnp.logical_and(i < n_used, lax.rem(i, n_buf) == s))(functools.partial(step, s))

    @pl.when(i >= n_used)
    def _():
        y_ref[...] = jnp.zeros_like(y_ref)


def _experts(n2p, inv, texp, n_used, w1t, b1c, w2, b2, layer, n_rows, tile):
    E, F2, D = w1t.shape[1:]
    F = F2 // 2
    n_tiles = n_rows // tile
    n_chunks = D // 2 // LANES
    slab = (tile * n_chunks, LANES)
    wsel = lambda i, inv, te, nu: (layer, te[i], 0, 0)
    grid_spec = pltpu.PrefetchScalarGridSpec(
        num_scalar_prefetch=3,
        grid=(n_tiles,),
        in_specs=[
            pl.BlockSpec(memory_space=pl.ANY),
            pl.BlockSpec((1, 1, F2, D), wsel),
            pl.BlockSpec((1, 1, 1, 2 * MXU_DIM), wsel),
            pl.BlockSpec((1, 1, F, D), wsel),
            pl.BlockSpec((1, 1, 1, D), wsel),
        ],
        out_specs=pl.BlockSpec(slab, lambda i, inv, te, nu: (i, 0)),
        scratch_shapes=[pltpu.VMEM(slab, U32), pltpu.VMEM(slab, U32), pltpu.VMEM(slab, U32),
                        pltpu.VMEM((2 * MXU_DIM, D), BF16), pltpu.VMEM((MXU_DIM, D), BF16),
                        pltpu.VMEM((D // LANES, F2, LANES), F32), pltpu.SemaphoreType.DMA((3,))],
    )
    return pl.pallas_call(
        functools.partial(_expert_body, tile=tile, d_expert=F),
        out_shape=jax.ShapeDtypeStruct((n_rows * n_chunks, LANES), U32),
        grid_spec=grid_spec,
        compiler_params=_cparams(("arbitrary",), 56),
        name="experts",
    )(inv, texp, n_used, n2p, w1t, b1c, w2, b2)


def _combine_body(dest_ref, y_hbm, wts_ref, h_ref, mod_ref, *rest, tile, n_tiles, gate_idx, with_norm):
    if with_norm:
        nab_ref, o_ref, n1_ref, ybuf0, ybuf1, sem = rest
    else:
        o_ref, ybuf0, ybuf1, sem = rest
        nab_ref = n1_ref = None
    i = pl.program_id(0)
    bufs = (ybuf0, ybuf1)
    n_groups = tile // SUBLANES
    n_chunks = ybuf0.shape[1] // tile
    half = n_chunks * LANES
    chunk = min(half, 8 * LANES)

    def group_copies(t, g, s):
        out = []
        for j in range(SUBLANES):
            for k in range(TOP_K):
                p = dest_ref[(t * tile + g * SUBLANES + j) * TOP_K + k]
                src = y_hbm.at[pl.ds(pl.multiple_of(p * n_chunks, n_chunks), n_chunks), :]
                dst = bufs[s].at[k, pl.ds(pl.multiple_of((g * SUBLANES + j) * n_chunks, n_chunks), n_chunks), :]
                out.append(pltpu.make_async_copy(src, dst, sem.at[s]))
        return out

    @pl.when(i == 0)
    def _():
        def f(g, c):
            for n, cp in enumerate(group_copies(0, g, 0)):
                cp.start(priority=n % 2)
            return c
        lax.fori_loop(0, n_groups, f, 0)

    def step(s, prefetch):
        def wait_group(g, c):
            for cp in group_copies(i, g, s):
                cp.wait()
            return c

        lax.fori_loop(0, n_groups, wait_group, 0)

        def mix_pair(gp, c):
            for u in range(2):
                g = 2 * gp + u
                if prefetch:
                    for n, cp in enumerate(group_copies(i + 1, g, 1 - s)):
                        cp.start(priority=n % 2)
                rows = pl.ds(pl.multiple_of(g * SUBLANES, SUBLANES), SUBLANES)
                w_blk = wts_ref[rows, :]
                ws = [w_blk[:, k:k + 1] for k in range(TOP_K)]
                first = g * (SUBLANES * n_chunks)
                for c in range(n_chunks):
                    acc_lo = None
                    acc_hi = None
                    for k in range(TOP_K):
                        words = bufs[s][k, pl.ds(first + c, SUBLANES, stride=n_chunks), :]
                        lo, hi = _unpack_bf16_pairs(words)
                        acc_lo = ws[k] * lo if acc_lo is None else acc_lo + ws[k] * lo
                        acc_hi = ws[k] * hi if acc_hi is None else acc_hi + ws[k] * hi
                    for base, acc in ((c * LANES, acc_lo), (half + c * LANES, acc_hi)):
                        gate = mod_ref[0, 0, pl.ds(gate_idx, 1), base:base + LANES]
                        o_ref[rows, base:base + LANES] = h_ref[rows, base:base + LANES] + gate * acc
            if n1_ref is not None:
                rows2 = pl.ds(pl.multiple_of(gp * 2 * SUBLANES, 2 * SUBLANES), 2 * SUBLANES)
                bases = range(0, 2 * half, chunk)
                ssq = None
                for base in bases:
                    x = o_ref[rows2, base:base + chunk]
                    part = jnp.sum(x * x, axis=-1, keepdims=True)
                    ssq = part if ssq is None else ssq + part
                rinv = lax.rsqrt(ssq * (1.0 / (2 * half)) + EPS)
                for base in bases:
                    a = nab_ref[0, 0, 0:1, base:base + chunk]
                    b = nab_ref[0, 0, 1:2, base:base + chunk]
                    x = o_ref[rows2, base:base + chunk]
                    n1_ref[rows2, base:base + chunk] = (x * rinv * a + b).astype(n1_ref.dtype)
            return c

        lax.fori_loop(0, n_groups // 2, mix_pair, 0)

    last = n_tiles - 1
    for s in range(2):
        pl.when(jnp.logical_and(i < last, lax.rem(i, 2) == s))(functools.partial(step, s, True))
    pl.when(i == last)(functools.partial(step, last % 2, False))


def _combine(ysp, dest, wts, h, mods, layer, group_of_tile, tile, gate_idx, norm_ab=None):
    T, D = h.shape
    half = D // 2
    n_tiles = T // tile
    with_norm = norm_ab is not None
    row_spec = pl.BlockSpec((tile, D), lambda i, d: (i, 0))
    in_specs = [
        pl.BlockSpec(memory_space=pl.ANY),
        pl.BlockSpec((tile, LANES), lambda i, d: (i, 0)),
        row_spec,
        pl.BlockSpec((1, 1, N_MOD, D), lambda i, d: (layer, group_of_tile(i), 0, 0)),
    ]
    operands = [dest, ysp, wts, h, mods]
    out_shape = jax.ShapeDtypeStruct((T, D), F32)
    out_specs = row_spec
    if with_norm:
        in_specs.append(pl.BlockSpec((1, 1, 2, D), lambda i, d: (layer + 1, group_of_tile(i), 0, 0)))
        operands.append(norm_ab)
        out_shape = (out_shape, jax.ShapeDtypeStruct((T, D), BF16))
        out_specs = (row_spec, row_spec)
    grid_spec = pltpu.PrefetchScalarGridSpec(
        num_scalar_prefetch=1,
        grid=(n_tiles,),
        in_specs=in_specs,
        out_specs=out_specs,
        scratch_shapes=[pltpu.VMEM((TOP_K, tile * (half // LANES), LANES), U32),
                        pltpu.VMEM((TOP_K, tile * (half // LANES), LANES), U32),
                        pltpu.SemaphoreType.DMA((2,))],
    )
    return pl.pallas_call(
        functools.partial(_combine_body, tile=tile, n_tiles=n_tiles, gate_idx=gate_idx, with_norm=with_norm),
        out_shape=out_shape,
        grid_spec=grid_spec,
        compiler_params=_cparams(("arbitrary",), 56),
        name="combine",
    )(*operands)


@jax.jit
def _forward(x, c, ctx, c_ctx, ada_w, ada_b, norm1_w, norm2_w, w_in, ret_decay, ret_gn_w,
             sg_norm_w, sg_w, sg_b, pool_w, pool_b, pool_scale, w_branch, gate_down, gate_up,
             gate_b, w_out, router_w, router_b, w1, b1, w2, b2, final_norm_w):
    B, n_lat, D = x.shape
    n_ctx = ctx.shape[1]
    depth = ada_w.shape[0]
    W = N_RET_HEADS * LANES
    E, _, F2 = w1.shape[1:]
    F = F2 // 2
    T = B * (n_lat + n_ctx)
    assert B + 1 <= COND_ROWS and n_lat % GRID_W == 0 and E == N_EXPERTS and F <= MXU_DIM
    assert n_lat % RET_CHUNK == 0 and n_ctx % RET_CHUNK == 0 and (B * n_lat) % n_ctx == 0

    tg = _pick(math.gcd(n_lat, B * n_ctx), (512, 256, 128))
    group_of = lambda rows: (lambda i: jnp.minimum((i * rows) // n_lat, B))
    tm = _pick(T, (1088, 1024, 768, 512, 384, 256, 128))
    tc = _pick(tg, (COMBINE_TILE, 128))

    h = jnp.concatenate([x.reshape(B * n_lat, D), ctx.reshape(B * n_ctx, D)], axis=0)
    cond = jnp.zeros((COND_ROWS, D), F32).at[:B].set(c).at[B].set(c_ctx)
    mods = _adaln(cond, ada_w, ada_b).reshape(depth, COND_ROWS, N_MOD, D)

    t = jnp.arange(n_lat)
    n_freq = LANES // 4
    freqs = ROPE_BASE ** (-jnp.arange(n_freq, dtype=F32) / n_freq)
    ang_r = (t // GRID_W).astype(F32)[:, None] * freqs
    ang_c = (t % GRID_W).astype(F32)[:, None] * freqs
    cos_t = jnp.concatenate([jnp.cos(ang_r)] * 2 + [jnp.cos(ang_c)] * 2, axis=-1)
    sin_t = jnp.concatenate([-jnp.sin(ang_r), jnp.sin(ang_r), -jnp.sin(ang_c), jnp.sin(ang_c)], axis=-1)

    blk = jnp.arange(T // SG_CHUNK)
    lat_blk, ctx_blk = n_lat // SG_CHUNK, n_ctx // SG_CHUNK
    in_lat = blk < B * lat_blk
    local = jnp.where(in_lat, blk % lat_blk, (blk - B * lat_blk) % ctx_blk)
    seq_blk = jnp.where(in_lat, lat_blk, ctx_blk)
    first_flag = (local == 0).astype(I32)
    last_flag = (local == seq_blk - 1).astype(I32)

    log_gamma = jax.nn.log_sigmoid(ret_decay.astype(F32))
    vec = lambda a: a.reshape(depth, 1, a.shape[-1])
    norm1_v, norm2_v, gn_v, sgn_v, pb_v, ps_v = map(vec, (norm1_w, norm2_w, ret_gn_w, sg_norm_w, pool_b, pool_scale))
    sgb_full = jnp.repeat(jnp.swapaxes(sg_b, 1, 2), LANES, axis=2)
    sg_wb = sg_w.astype(BF16)
    pool_wb = pool_w.astype(BF16)
    gate_b4 = gate_b.reshape(depth, 3, 1, D)
    router_wp = jnp.zeros((depth, D, LANES), BF16).at[:, :, :E].set(router_w.astype(BF16))
    router_bp = jnp.zeros((depth, 1, LANES), F32).at[:, 0, :E].set(router_b)
    w1t = jnp.swapaxes(w1, 2, 3)
    b1c = (jnp.zeros((depth, E, 1, 2 * MXU_DIM), F32).at[:, :, 0, :F].set(b1[:, :, 0::2])
           .at[:, :, 0, MXU_DIM:MXU_DIM + F].set(b1[:, :, 1::2]))
    b2_4 = b2.reshape(depth, E, 1, D)

    n_pairs = T * TOP_K
    n_rows = -(-(n_pairs + E * (EXPERT_TILE - 1)) // EXPERT_TILE) * EXPERT_TILE
    tile_start = jnp.arange(n_rows // EXPERT_TILE, dtype=I32) * EXPERT_TILE
    expert_ids = jnp.arange(E, dtype=I32)

    norm_ab = jnp.stack([norm1_w[:, None, :] * (1.0 + mods[:, :, 1, :]), mods[:, :, 0, :]], axis=2)

    n1 = _norm_mod(h, norm1_v, mods, 0, group_of(tg), tg, 0, 1)
    for l in range(depth):
        z = _matmul(n1, w_in, l, tm, 512)
        r = _matmul(n1, gate_down, l, tm, gate_down.shape[2])
        ya_x, ya_c = _retention(z, log_gamma, cos_t, sin_t, gn_v, l, B, n_lat, n_ctx)
        yb, yc = _branches_bc(z, first_flag, last_flag, sgn_v, sg_wb, sgb_full, pool_wb, pb_v, ps_v, l)
        ya = jnp.concatenate([ya_x, ya_c], axis=0)
        m = _merge(r, ya, yb, yc, gate_up, gate_b4, w_branch, l, tm, 512)
        h = _matmul_residual(m, w_out, h, mods, l, n_lat, B + 1, tm, 512, 2)

        n2p, meta, wts, counts = _norm_router(h, norm2_v, mods, router_wp, router_bp, l, group_of(tg), tg, 3, 4)
        cnt = counts[0, :E].astype(I32)
        padded = ((cnt + EXPERT_TILE - 1) // EXPERT_TILE) * EXPERT_TILE
        po = jnp.concatenate([jnp.zeros((1,), I32), jnp.cumsum(padded, dtype=I32)])
        texp = jnp.minimum(jnp.sum(tile_start[:, None] >= po[None, 1:], axis=1, dtype=I32), E - 1)
        n_used = (po[E] // EXPERT_TILE).reshape(1)
        ids = meta[:, :TOP_K]
        start_of = jnp.sum(jnp.where(ids[:, :, None] == expert_ids, po[:E], 0), axis=-1, dtype=I32)
        dest = (start_of + meta[:, TOP_K:2 * TOP_K]).reshape(n_pairs)
        inv = _row_token(dest, po, cnt, n_rows)
        ysp = _experts(n2p, inv, texp, n_used, w1t, b1c, w2, b2_4, l, n_rows, EXPERT_TILE)
        if l + 1 < depth:
            h, n1 = _combine(ysp, dest, wts, h, mods, l, group_of(tc), tc, 5, norm_ab)
        else:
            h = _combine(ysp, dest, wts, h, mods, l, group_of(tc), tc, 5)

    out = _final_norm(h, final_norm_w, B * n_lat, tg)
    return out.reshape(B, n_lat, D)


def kernel(x, c, ctx, c_ctx, ada_w, ada_b, norm1_w, norm2_w, w_in, ret_decay, ret_gn_w, sg_norm_w, sg_w, sg_b, pool_w, pool_b, pool_scale, w_branch, gate_down, gate_up, gate_b, w_out, router_w, router_b, w1, b1, w2, b2, final_norm_w):
    return _forward(x, c, ctx, c_ctx, ada_w, ada_b, norm1_w, norm2_w, w_in, ret_decay, ret_gn_w,
                    sg_norm_w, sg_w, sg_b, pool_w, pool_b, pool_scale, w_branch, gate_down, gate_up,
                    gate_b, w_out, router_w, router_b, w1, b1, w2, b2, final_norm_w)
```

```python
import functools
import math

import jax
import jax.numpy as jnp
from jax import lax
from jax.experimental import pallas as pl
from jax.experimental.pallas import tpu as pltpu

F32 = jnp.float32
BF16 = jnp.bfloat16
U32 = jnp.uint32
I32 = jnp.int32

GRID_W = 64
N_RET_HEADS = 8
RET_CHUNK = 128
ROPE_BASE = 10000.0
SG_CHUNK = 128
SG_GROUPS = 8
POOL_WINDOWS = (2, 4, 8, 16)
N_EXPERTS = 32
TOP_K = 4
SWIGLU_LIMIT = 7.0
SWIGLU_ALPHA = 1.702
N_MOD = 6
EPS = 1e-6

LANES = 128
SUBLANES = 8
MXU_DIM = 256
COND_ROWS = 8
NEG_BIG = -1e30
EXPERT_TILE = 256
COMBINE_TILE = 256
MIB = 1024 * 1024


def _cparams(sem, vmem_mib):
    return pltpu.CompilerParams(dimension_semantics=sem, vmem_limit_bytes=int(vmem_mib * MIB))


def _pick(n, cands):
    for c in cands:
        if n % c == 0:
            return c
    raise ValueError(f"no tile in {cands} divides {n}")


def _dot(a, b):
    return jnp.dot(a, b, preferred_element_type=F32)


def _dot_tn(a, b):
    return lax.dot_general(a, b, (((0,), (0,)), ((), ())), preferred_element_type=F32)


def _dot_nt(a, b):
    return lax.dot_general(a, b, (((1,), (1,)), ((), ())), preferred_element_type=F32)


def _sigmoid(x):
    return 1.0 / (1.0 + jnp.exp(-x))


def _gelu_tanh(x):
    return 0.5 * x * (1.0 + jnp.tanh(math.sqrt(2.0 / math.pi) * (x + 0.044715 * (x * x * x))))


def _pack_bf16_pairs(x):
    n = x.shape[1] // 2
    xb = x.astype(BF16).astype(F32)
    lo = lax.shift_right_logical(lax.bitcast_convert_type(xb[:, :n], U32), jnp.uint32(16))
    hi = lax.bitcast_convert_type(xb[:, n:], U32) & jnp.uint32(0xFFFF0000)
    return lo | hi


def _unpack_bf16_pairs(w):
    lo = lax.bitcast_convert_type(lax.shift_left(w, jnp.uint32(16)), F32)
    hi = lax.bitcast_convert_type(w & jnp.uint32(0xFFFF0000), F32)
    return lo, hi


def _store_slabs(ref, words):
    rows = words.shape[0]
    n_chunks = words.shape[1] // LANES
    for c in range(n_chunks):
        ref[pl.ds(c, rows, stride=n_chunks), :] = words[:, c * LANES:(c + 1) * LANES]


def _load_slabs(ref, rows, n_chunks):
    return jnp.concatenate([ref[pl.ds(c, rows, stride=n_chunks), :] for c in range(n_chunks)], axis=1)


def _adaln_body(cond_ref, w_ref, b_ref, o_ref):
    s = cond_ref[...]
    s = (s * _sigmoid(s)).astype(BF16)
    o_ref[0] = _dot(s, w_ref[0].astype(BF16)) + b_ref[0]


def _adaln(cond, ada_w, ada_b):
    L, D, M = ada_w.shape
    tn = _pick(M, (512, 256, 128))
    return pl.pallas_call(
        _adaln_body,
        out_shape=jax.ShapeDtypeStruct((L, COND_ROWS, M), F32),
        grid=(L, M // tn),
        in_specs=[
            pl.BlockSpec((COND_ROWS, D), lambda l, j: (0, 0)),
            pl.BlockSpec((1, D, tn), lambda l, j: (l, 0, j)),
            pl.BlockSpec((1, 1, tn), lambda l, j: (l, 0, j)),
        ],
        out_specs=pl.BlockSpec((1, COND_ROWS, tn), lambda l, j: (l, 0, j)),
        compiler_params=_cparams(("arbitrary", "arbitrary"), 40),
        name="adaln",
    )(cond, ada_w, ada_b.reshape(L, 1, M))


def _rms_modulate(x, w, mod_ref, layer, shift_idx, scale_idx):
    ms = jnp.mean(x * x, axis=-1, keepdims=True)
    y = x * lax.rsqrt(ms + EPS) * w
    shift = mod_ref[0, 0, pl.ds(shift_idx, 1), :]
    scale = mod_ref[0, 0, pl.ds(scale_idx, 1), :]
    return y * (1.0 + scale) + shift


def _norm_mod_body(h_ref, w_ref, mod_ref, o_ref, *, layer, shift_idx, scale_idx):
    o_ref[...] = _rms_modulate(h_ref[...], w_ref[0], mod_ref, layer, shift_idx, scale_idx).astype(o_ref.dtype)


def _norm_mod(h, w, mods, layer, group_of_tile, tm, shift_idx, scale_idx):
    T, D = h.shape
    return pl.pallas_call(
        functools.partial(_norm_mod_body, layer=layer, shift_idx=shift_idx, scale_idx=scale_idx),
        out_shape=jax.ShapeDtypeStruct((T, D), BF16),
        grid=(T // tm,),
        in_specs=[
            pl.BlockSpec((tm, D), lambda i: (i, 0)),
            pl.BlockSpec((1, 1, D), lambda i: (layer, 0, 0)),
            pl.BlockSpec((1, 1, N_MOD, D), lambda i: (layer, group_of_tile(i), 0, 0)),
        ],
        out_specs=pl.BlockSpec((tm, D), lambda i: (i, 0)),
        compiler_params=_cparams(("arbitrary",), 40),
        name="norm_mod",
    )(h, w, mods)


def _final_norm_body(h_ref, w_ref, o_ref):
    x = h_ref[...]
    ms = jnp.mean(x * x, axis=-1, keepdims=True)
    o_ref[...] = x * lax.rsqrt(ms + EPS) * w_ref[...]


def _final_norm(h, w, rows, tm):
    D = h.shape[1]
    return pl.pallas_call(
        _final_norm_body,
        out_shape=jax.ShapeDtypeStruct((rows, D), F32),
        grid=(rows // tm,),
        in_specs=[pl.BlockSpec((tm, D), lambda i: (i, 0)), pl.BlockSpec((1, D), lambda i: (0, 0))],
        out_specs=pl.BlockSpec((tm, D), lambda i: (i, 0)),
        compiler_params=_cparams(("arbitrary",), 40),
        name="final_norm",
    )(h, w.reshape(1, D))


def _mm_body(x_ref, w_ref, o_ref, wb_ref):
    @pl.when(pl.program_id(1) == 0)
    def _():
        wb_ref[...] = w_ref[0].astype(BF16)

    o_ref[...] = _dot(x_ref[...], wb_ref[...]).astype(o_ref.dtype)


def _matmul(x, w, layer, tm, tn, out_dtype=BF16):
    T, K = x.shape
    N = w.shape[2]
    return pl.pallas_call(
        _mm_body,
        out_shape=jax.ShapeDtypeStruct((T, N), out_dtype),
        grid=(N // tn, T // tm),
        in_specs=[
            pl.BlockSpec((tm, K), lambda j, i: (i, 0)),
            pl.BlockSpec((1, K, tn), lambda j, i: (layer, 0, j)),
        ],
        out_specs=pl.BlockSpec((tm, tn), lambda j, i: (i, j)),
        scratch_shapes=[pltpu.VMEM((K, tn), BF16)],
        compiler_params=_cparams(("arbitrary", "arbitrary"), 56),
        name="matmul",
    )(x, w)


def _mm_res_body(x_ref, w_ref, h_ref, mod_ref, o_ref, wb_ref, *, gate_idx, n_lat, n_groups):
    @pl.when(pl.program_id(1) == 0)
    def _():
        wb_ref[...] = w_ref[0].astype(BF16)

    tm = o_ref.shape[0]
    row = pl.program_id(1) * tm + lax.broadcasted_iota(I32, (tm, 1), 0)
    gate = mod_ref[0, n_groups - 1, pl.ds(gate_idx, 1), :]
    for b in reversed(range(n_groups - 1)):
        gate = jnp.where(row < (b + 1) * n_lat, mod_ref[0, b, pl.ds(gate_idx, 1), :], gate)
    o_ref[...] = h_ref[...] + gate * _dot(x_ref[...], wb_ref[...])


def _matmul_residual(x, w, h, mods, layer, n_lat, n_groups, tm, tn, gate_idx):
    T, K = x.shape
    N = w.shape[2]
    return pl.pallas_call(
        functools.partial(_mm_res_body, gate_idx=gate_idx, n_lat=n_lat, n_groups=n_groups),
        out_shape=jax.ShapeDtypeStruct((T, N), F32),
        grid=(N // tn, T // tm),
        in_specs=[
            pl.BlockSpec((tm, K), lambda j, i: (i, 0)),
            pl.BlockSpec((1, K, tn), lambda j, i: (layer, 0, j)),
            pl.BlockSpec((tm, tn), lambda j, i: (i, j)),
            pl.BlockSpec((1, COND_ROWS, N_MOD, tn), lambda j, i: (layer, 0, 0, j)),
        ],
        out_specs=pl.BlockSpec((tm, tn), lambda j, i: (i, j)),
        scratch_shapes=[pltpu.VMEM((K, tn), BF16)],
        compiler_params=_cparams(("arbitrary", "arbitrary"), 56),
        name="matmul_residual",
    )(x, w, h, mods)


def _retention_body(lg_ref, qx_ref, kx_ref, vx_ref, gx_ref, qc_ref, kc_ref, vc_ref, gc_ref,
                    cos_ref, sin_ref, gnw_ref, ox_ref, oc_ref, qr_ref, kr_ref, u_ref, st_ref,
                    *, layer, nch_x, nch_c, k_scale):
    C = RET_CHUNK
    head = pl.program_id(1)
    lgf = lg_ref[layer, 0, head]
    lgb = lg_ref[layer, 1, head]

    ri = lax.broadcasted_iota(I32, (C, C), 0)
    ci = lax.broadcasted_iota(I32, (C, C), 1)
    diff = (ri - ci).astype(F32)
    dmask = (jnp.where(diff >= 0.0, jnp.exp(lgf * jnp.maximum(diff, 0.0)), 0.0)
             + jnp.where(diff <= 0.0, jnp.exp(lgb * jnp.maximum(-diff, 0.0)), 0.0))
    pos = lax.broadcasted_iota(I32, (C, 1), 0).astype(F32)
    q_dec_f = jnp.exp(lgf * (pos + 1.0))
    k_dec_f = jnp.exp(lgf * (C - 1.0 - pos))
    q_dec_b = jnp.exp(lgb * (C - pos))
    k_dec_b = jnp.exp(lgb * pos)
    full_c = jnp.full((C, C), float(C), F32)
    chunk_dec_f = jnp.exp(lgf * full_c)
    chunk_dec_b = jnp.exp(lgb * full_c)
    swap = jnp.where((ri ^ 32) == ci, 1.0, 0.0).astype(BF16)
    gnw = gnw_ref[0]

    def rows_of(c):
        return pl.ds(pl.multiple_of(c * C, C), C)

    def rope_chunk(c, carry):
        rows = rows_of(c)
        cos = cos_ref[rows, :]
        sin = sin_ref[rows, :]
        q = qx_ref[rows, :]
        k = kx_ref[rows, :]
        qr_ref[rows, :] = (q.astype(F32) * cos + _dot(q, swap) * sin).astype(BF16)
        kr_ref[rows, :] = ((k.astype(F32) * cos + _dot(k, swap) * sin) * k_scale).astype(BF16)
        return carry

    lax.fori_loop(0, nch_x, rope_chunk, 0, unroll=min(4, nch_x))

    def segment(q_at, k_at, v_at, g_ref, o_ref, nch, s_f0, s_b0):
        def incr(c, carry):
            kf = k_at(c).astype(F32)
            kd = jnp.concatenate([kf * k_dec_f, kf * k_dec_b], axis=1).astype(BF16)
            u_ref[c] = _dot_tn(kd, v_at(c))
            return carry

        lax.fori_loop(0, nch, incr, 0, unroll=min(4, nch))

        def fwd_scan(c, s):
            st_ref[c, 0:C, :] = s.astype(BF16)
            return chunk_dec_f * s + u_ref[c, 0:C, :]

        s_f_final = lax.fori_loop(0, nch, fwd_scan, s_f0)

        def bwd_scan(t, s):
            c = nch - 1 - t
            st_ref[c, C:2 * C, :] = s.astype(BF16)
            return chunk_dec_b * s + u_ref[c, C:2 * C, :]

        s_b_final = lax.fori_loop(0, nch, bwd_scan, s_b0)

        def emit(c, carry):
            rows = rows_of(c)
            q = q_at(c)
            v = v_at(c)
            qf = q.astype(F32)
            scores = _dot_nt(q, k_at(c)) * dmask
            qd = jnp.concatenate([qf * q_dec_f, qf * q_dec_b], axis=1).astype(BF16)
            y = _dot(scores.astype(BF16), v) + _dot(qd, st_ref[c])
            mu = jnp.mean(y, axis=-1, keepdims=True)
            d = y - mu
            var = jnp.mean(d * d, axis=-1, keepdims=True)
            yn = d * lax.rsqrt(var + EPS) * gnw
            g = g_ref[rows, :].astype(F32)
            o_ref[rows, :] = (g * _sigmoid(g) * yn).astype(o_ref.dtype)
            return carry

        lax.fori_loop(0, nch, emit, 0, unroll=min(4, nch))
        return s_f_final, s_b_final

    def at(ref, scale=None):
        def get(c):
            x = ref[rows_of(c), :]
            if scale is not None:
                x = (x.astype(F32) * scale).astype(BF16)
            return x
        return get

    zero = jnp.zeros((C, C), F32)
    s_f, s_b = segment(at(qc_ref), at(kc_ref, k_scale), at(vc_ref), gc_ref, oc_ref, nch_c, zero, zero)
    segment(at(qr_ref), at(kr_ref), at(vx_ref), gx_ref, ox_ref, nch_x, s_f, s_b)


def _retention(z, log_gamma, cos_t, sin_t, gn_w, layer, B, n_lat, n_ctx):
    W = N_RET_HEADS * LANES
    H = N_RET_HEADS
    ctx0 = (B * n_lat) // n_ctx

    def xspec(col):
        return pl.BlockSpec((n_lat, LANES), lambda b, h: (b, col * H + h))

    def cspec(col):
        return pl.BlockSpec((n_ctx, LANES), lambda b, h: (ctx0 + b, col * H + h))

    nch_x = n_lat // RET_CHUNK
    nch_c = n_ctx // RET_CHUNK
    nch = max(nch_x, nch_c)
    return pl.pallas_call(
        functools.partial(_retention_body, layer=layer, nch_x=nch_x, nch_c=nch_c, k_scale=float(LANES) ** -0.5),
        out_shape=(jax.ShapeDtypeStruct((B * n_lat, W), BF16), jax.ShapeDtypeStruct((B * n_ctx, W), BF16)),
        grid=(B, H),
        in_specs=[
            pl.BlockSpec(memory_space=pltpu.SMEM),
            xspec(0), xspec(1), xspec(2), xspec(3),
            cspec(0), cspec(1), cspec(2), cspec(3),
            pl.BlockSpec((n_lat, LANES), lambda b, h: (0, 0)),
            pl.BlockSpec((n_lat, LANES), lambda b, h: (0, 0)),
            pl.BlockSpec((1, 1, LANES), lambda b, h: (layer, 0, h)),
        ],
        out_specs=(pl.BlockSpec((n_lat, LANES), lambda b, h: (b, h)),
                   pl.BlockSpec((n_ctx, LANES), lambda b, h: (b, h))),
        scratch_shapes=[pltpu.VMEM((n_lat, LANES), BF16), pltpu.VMEM((n_lat, LANES), BF16),
                        pltpu.VMEM((nch, 2 * RET_CHUNK, RET_CHUNK), F32),
                        pltpu.VMEM((nch, 2 * RET_CHUNK, RET_CHUNK), BF16)],
        compiler_params=_cparams(("arbitrary", "arbitrary"), 48),
        name="retention",
    )(log_gamma, z, z, z, z, z, z, z, z, cos_t, sin_t, gn_w)


def _branch_body(first_ref, last_ref, u_ref, v_ref, pp_ref, pc_ref, pn_ref, sgn_ref, sgw_ref, sgb_ref,
                 pw_ref, pb_ref, ps_ref, yb_ref, yc_ref):
    C = SG_CHUNK
    i = pl.program_id(0)
    is_first = first_ref[i] == 1
    is_last = last_ref[i] == 1

    u = _gelu_tanh(u_ref[...].astype(F32))
    v = _gelu_tanh(v_ref[...].astype(F32))
    mu = jnp.mean(v, axis=-1, keepdims=True)
    d = v - mu
    var = jnp.mean(d * d, axis=-1, keepdims=True)
    vn = (d * lax.rsqrt(var + EPS) * sgn_ref[0]).astype(BF16)
    for g in range(SG_GROUPS):
        cols = slice(g * LANES, (g + 1) * LANES)
        mixed = _dot(sgw_ref[0, g], vn[:, cols]) + sgb_ref[0, :, cols]
        yb_ref[:, cols] = (u[:, cols] * mixed).astype(yb_ref.dtype)

    r = lax.broadcasted_iota(I32, (C, C), 0)
    s = lax.broadcasted_iota(I32, (C, C), 1)
    rr = lax.broadcasted_iota(I32, (C, 1), 0)
    gw = pc_ref.shape[1] // len(POOL_WINDOWS)
    for gi, w in enumerate(POOL_WINDOWS):
        hw = w // 2
        cols = slice(gi * gw, (gi + 1) * gw)
        a_mid = jnp.where((s >= r - hw) & (s < r + hw), 1.0, 0.0)
        a_prev = jnp.where(is_first, 0.0, jnp.where(s >= r - hw + C, 1.0, 0.0))
        a_next = jnp.where(is_last, 0.0, jnp.where(s < r + hw - C, 1.0, 0.0))
        lo = jnp.where(is_first, jnp.maximum(rr - hw, 0), rr - hw)
        hi = jnp.where(is_last, jnp.minimum(rr + hw, C), rr + hw)
        cnt = (hi - lo).astype(F32)
        pg = pc_ref[:, cols]
        tot = (_dot(a_mid.astype(BF16), pg) + _dot(a_prev.astype(BF16), pp_ref[:, cols])
               + _dot(a_next.astype(BF16), pn_ref[:, cols]))
        pooled = tot / cnt - pg.astype(F32)
        yc = (_dot(pooled.astype(BF16), pw_ref[0, gi]) + pb_ref[0, :, cols]) * ps_ref[0, :, cols]
        yc_ref[:, cols] = yc.astype(yc_ref.dtype)


def _branches_bc(z, first_flag, last_flag, sg_norm_w, sg_w, sgb_full, pool_w, pool_b, pool_scale, layer):
    T = z.shape[0]
    W = N_RET_HEADS * LANES
    C = SG_CHUNK
    nblk = T // C
    row = lambda col: pl.BlockSpec((C, W), lambda i, f, l: (i, col))
    vec = pl.BlockSpec((1, 1, W), lambda i, f, l: (layer, 0, 0))
    grid_spec = pltpu.PrefetchScalarGridSpec(
        num_scalar_prefetch=2,
        grid=(nblk,),
        in_specs=[
            row(4), row(5),
            pl.BlockSpec((C, W), lambda i, f, l: (jnp.maximum(i - 1, 0), 6)),
            row(6),
            pl.BlockSpec((C, W), lambda i, f, l: (jnp.minimum(i + 1, nblk - 1), 6)),
            vec,
            pl.BlockSpec((1, SG_GROUPS, C, C), lambda i, f, l: (layer, 0, 0, 0)),
            pl.BlockSpec((1, C, W), lambda i, f, l: (layer, 0, 0)),
            pl.BlockSpec((1,) + pool_w.shape[1:], lambda i, f, l: (layer, 0, 0, 0)),
            vec, vec,
        ],
        out_specs=(pl.BlockSpec((C, W), lambda i, f, l: (i, 0)), pl.BlockSpec((C, W), lambda i, f, l: (i, 0))),
    )
    return pl.pallas_call(
        _branch_body,
        out_shape=(jax.ShapeDtypeStruct((T, W), BF16), jax.ShapeDtypeStruct((T, W), BF16)),
        grid_spec=grid_spec,
        compiler_params=_cparams(("arbitrary",), 32),
        name="branches_bc",
    )(first_flag, last_flag, z, z, z, z, z, sg_norm_w, sg_w, sgb_full, pool_w, pool_b, pool_scale)


def _merge_body(r_ref, ya_ref, yb_ref, yc_ref, gu_ref, gb_ref, wbr_ref, o_ref, gub_ref, wbb_ref):
    @pl.when(pl.program_id(1) == 0)
    def _():
        gub_ref[...] = gu_ref[0].astype(BF16)
        wbb_ref[...] = wbr_ref[0].astype(BF16)

    r = r_ref[...]
    acc = None
    for k, y_ref in enumerate((ya_ref, yb_ref, yc_ref)):
        gate = _sigmoid(_dot(r, gub_ref[k]) + gb_ref[0, k])
        term = gate * _dot(y_ref[...], wbb_ref[k])
        acc = term if acc is None else acc + term
    o_ref[...] = acc.astype(o_ref.dtype)


def _merge(r, ya, yb, yc, gate_up, gate_b, w_branch, layer, tm, tn):
    T, R = r.shape
    W = yb.shape[1]
    D = gate_up.shape[3]
    yspec = pl.BlockSpec((tm, W), lambda j, i: (i, 0))
    return pl.pallas_call(
        _merge_body,
        out_shape=jax.ShapeDtypeStruct((T, D), BF16),
        grid=(D // tn, T // tm),
        in_specs=[
            pl.BlockSpec((tm, R), lambda j, i: (i, 0)),
            yspec, yspec, yspec,
            pl.BlockSpec((1, 3, R, tn), lambda j, i: (layer, 0, 0, j)),
            pl.BlockSpec((1, 3, 1, tn), lambda j, i: (layer, 0, 0, j)),
            pl.BlockSpec((1, 3, W, tn), lambda j, i: (layer, 0, 0, j)),
        ],
        out_specs=pl.BlockSpec((tm, tn), lambda j, i: (i, j)),
        scratch_shapes=[pltpu.VMEM((3, R, tn), BF16), pltpu.VMEM((3, W, tn), BF16)],
        compiler_params=_cparams(("arbitrary", "arbitrary"), 56),
        name="merge",
    )(r, ya, yb, yc, gate_up, gate_b, w_branch)


def _norm_router_body(h_ref, w_ref, mod_ref, rw_ref, rb_ref, n2p_ref, meta_ref, wts_ref, cnt_ref, run_ref,
                      *, layer, shift_idx, scale_idx):
    i = pl.program_id(0)

    @pl.when(i == 0)
    def _():
        run_ref[...] = jnp.zeros_like(run_ref)

    n2 = _rms_modulate(h_ref[...], w_ref[0], mod_ref, layer, shift_idx, scale_idx)
    _store_slabs(n2p_ref, _pack_bf16_pairs(n2))

    logits = _dot(n2.astype(BF16), rw_ref[0]) + rb_ref[0]
    tr = logits.shape[0]
    lane = lax.broadcasted_iota(I32, logits.shape, 1)
    work = jnp.where(lane < N_EXPERTS, logits, NEG_BIG)
    vals, hots, idxs = [], [], []
    for _ in range(TOP_K):
        m = jnp.max(work, axis=-1, keepdims=True)
        idx = jnp.min(jnp.where(work == m, lane, LANES), axis=-1, keepdims=True)
        hot = lane == idx
        vals.append(m)
        hots.append(hot)
        idxs.append(idx)
        work = jnp.where(hot, NEG_BIG, work)
    es = [jnp.exp(v - vals[0]) for v in vals]
    inv = 1.0 / (es[0] + es[1] + es[2] + es[3])

    hot_all = jnp.zeros(logits.shape, F32)
    for hot in hots:
        hot_all = hot_all + jnp.where(hot, 1.0, 0.0)
    rr = lax.broadcasted_iota(I32, (tr, tr), 0)
    cc = lax.broadcasted_iota(I32, (tr, tr), 1)
    earlier = jnp.where(rr > cc, 1.0, 0.0).astype(BF16)
    base = _dot(earlier, hot_all.astype(BF16)) + run_ref[...]

    meta = jnp.zeros(logits.shape, I32)
    wts = jnp.zeros(logits.shape, F32)
    for k in range(TOP_K):
        rank = jnp.sum(jnp.where(hots[k], base, 0.0), axis=-1, keepdims=True).astype(I32)
        meta = jnp.where(lane == k, idxs[k], meta)
        meta = jnp.where(lane == TOP_K + k, rank, meta)
        wts = jnp.where(lane == k, es[k] * inv, wts)
    meta_ref[...] = meta
    wts_ref[...] = wts
    run_ref[...] = run_ref[...] + jnp.sum(hot_all, axis=0, keepdims=True)
    cnt_ref[...] = jnp.broadcast_to(run_ref[...], cnt_ref.shape)


def _norm_router(h, norm_w, mods, router_wp, router_bp, layer, group_of_tile, tm, shift_idx, scale_idx):
    T, D = h.shape
    return pl.pallas_call(
        functools.partial(_norm_router_body, layer=layer, shift_idx=shift_idx, scale_idx=scale_idx),
        out_shape=(jax.ShapeDtypeStruct((T * (D // 2 // LANES), LANES), U32), jax.ShapeDtypeStruct((T, LANES), I32),
                   jax.ShapeDtypeStruct((T, LANES), F32), jax.ShapeDtypeStruct((COND_ROWS, LANES), F32)),
        grid=(T // tm,),
        in_specs=[
            pl.BlockSpec((tm, D), lambda i: (i, 0)),
            pl.BlockSpec((1, 1, D), lambda i: (layer, 0, 0)),
            pl.BlockSpec((1, 1, N_MOD, D), lambda i: (layer, group_of_tile(i), 0, 0)),
            pl.BlockSpec((1, D, LANES), lambda i: (layer, 0, 0)),
            pl.BlockSpec((1, 1, LANES), lambda i: (layer, 0, 0)),
        ],
        out_specs=(pl.BlockSpec((tm * (D // 2 // LANES), LANES), lambda i: (i, 0)),
                   pl.BlockSpec((tm, LANES), lambda i: (i, 0)),
                   pl.BlockSpec((tm, LANES), lambda i: (i, 0)), pl.BlockSpec((COND_ROWS, LANES), lambda i: (0, 0))),
        scratch_shapes=[pltpu.VMEM((1, LANES), F32)],
        compiler_params=_cparams(("arbitrary",), 48),
        name="norm_router",
    )(h, norm_w, mods, router_wp, router_bp)


def _row_token_body(dest_ref, po_ref, cnt_ref, inv_ref, *, n_pairs, n_rows):
    def pad_expert(e, carry):
        def f(p, c):
            inv_ref[p] = 0
            return c
        lax.fori_loop(po_ref[e] + cnt_ref[e], po_ref[e + 1], f, 0)
        return carry

    lax.fori_loop(0, N_EXPERTS, pad_expert, 0)

    def tail(p, c):
        inv_ref[p] = 0
        return c

    lax.fori_loop(po_ref[N_EXPERTS], n_rows, tail, 0)

    def scatter(a, c):
        inv_ref[dest_ref[a]] = lax.shift_right_logical(a, TOP_K.bit_length() - 1)
        return c

    lax.fori_loop(0, n_pairs, scatter, 0, unroll=8)


def _row_token(dest, po, cnt, n_rows):
    smem = pl.BlockSpec(memory_space=pltpu.SMEM)
    return pl.pallas_call(
        functools.partial(_row_token_body, n_pairs=dest.shape[0], n_rows=n_rows),
        out_shape=jax.ShapeDtypeStruct((n_rows,), I32),
        in_specs=[smem, smem, smem],
        out_specs=smem,
        name="row_token",
    )(dest, po, cnt)


def _expert_body(inv_ref, texp_ref, nused_ref, x_hbm, w1t_ref, b1_ref, w2_ref, b2_ref, y_ref,
                 xbuf, w1c, w2b, stage, sem, *, tile, d_expert):
    i = pl.program_id(0)
    n_used = nused_ref[0]
    n_buf = xbuf.shape[0]
    n_chunks = xbuf.shape[1] // tile

    def row_copy(t, r, s):
        tok = inv_ref[t * tile + r]
        src = x_hbm.at[pl.ds(pl.multiple_of(tok * n_chunks, n_chunks), n_chunks), :]
        dst = xbuf.at[s, pl.ds(pl.multiple_of(r * n_chunks, n_chunks), n_chunks), :]
        return pltpu.make_async_copy(src, dst, sem.at[s])

    def start_tile(t, s):
        for r in range(tile):
            row_copy(t, r, s).start(priority=r % 2)

    def start_tile_rolled(t, s):
        def f(r, c):
            row_copy(t, r, s).start()
            return c
        lax.fori_loop(0, tile, f, 0, unroll=8)

    def wait_tile(t, s):
        def f(r, c):
            row_copy(t, r, s).wait()
            return c
        lax.fori_loop(0, tile, f, 0, unroll=8)

    last = n_used - 1

    @pl.when(i == 0)
    def _():
        w1c[...] = jnp.zeros_like(w1c)
        w2b[...] = jnp.zeros_like(w2b)
        start_tile_rolled(0, 0)
        start_tile_rolled(jnp.minimum(1, last), 1)

    @pl.when(i < n_used)
    def _():
        s = lax.rem(i, n_buf)
        changed = jnp.logical_or(i == 0, texp_ref[i] != texp_ref[jnp.maximum(i - 1, 0)])

        @pl.when(changed)
        def _():
            for kc in range(stage.shape[0]):
                cols = slice(kc * LANES, (kc + 1) * LANES)
                stage[kc] = w1t_ref[0, 0, :, cols]
                w1c[0:d_expert, cols] = stage[kc, pl.ds(0, d_expert, stride=2), :].astype(BF16)
                w1c[MXU_DIM:MXU_DIM + d_expert, cols] = stage[kc, pl.ds(1, d_expert, stride=2), :].astype(BF16)
            w2b[0:d_expert, :] = w2_ref[0, 0].astype(BF16)

        wait_tile(i, s)
        nxt = jnp.minimum(i + 2, last)
        start_tile(nxt, lax.rem(i + 2, n_buf))
        lo, hi = _unpack_bf16_pairs(_load_slabs(xbuf.at[s], tile, n_chunks))
        half = lo.shape[1]
        gp = (_dot_nt(lo.astype(BF16), w1c[:, 0:half]) + _dot_nt(hi.astype(BF16), w1c[:, half:2 * half])
              + b1_ref[0, 0])
        glu = jnp.minimum(gp[:, :MXU_DIM], SWIGLU_LIMIT)
        lin = jnp.clip(gp[:, MXU_DIM:], -SWIGLU_LIMIT, SWIGLU_LIMIT)
        act = glu * _sigmoid(SWIGLU_ALPHA * glu) * (lin + 1.0)
        y = _dot(act.astype(BF16), w2b[...]) + b2_ref[0, 0]
        _store_slabs(y_ref, _pack_bf16_pairs(y))

        @pl.when(i == last)
        def _():
            wait_tile(last, lax.rem(i + 1, n_buf))
            wait_tile(last, lax.rem(i + 2, n_buf))

    @pl.when(i >= n_used)
    def _():
        y_ref[...] = jnp.zeros_like(y_ref)


def _experts(n2p, inv, texp, n_used, w1t, b1c, w2, b2, layer, n_rows, tile):
    E, F2, D = w1t.shape[1:]
    F = F2 // 2
    n_tiles = n_rows // tile
    n_chunks = D // 2 // LANES
    slab = (tile * n_chunks, LANES)
    wsel = lambda i, inv, te, nu: (layer, te[i], 0, 0)
    grid_spec = pltpu.PrefetchScalarGridSpec(
        num_scalar_prefetch=3,
        grid=(n_tiles,),
        in_specs=[
            pl.BlockSpec(memory_space=pl.ANY),
            pl.BlockSpec((1, 1, F2, D), wsel),
            pl.BlockSpec((1, 1, 1, 2 * MXU_DIM), wsel),
            pl.BlockSpec((1, 1, F, D), wsel),
            pl.BlockSpec((1, 1, 1, D), wsel),
        ],
        out_specs=pl.BlockSpec(slab, lambda i, inv, te, nu: (i, 0)),
        scratch_shapes=[pltpu.VMEM((3,) + slab, U32),
                        pltpu.VMEM((2 * MXU_DIM, D), BF16), pltpu.VMEM((MXU_DIM, D), BF16),
                        pltpu.VMEM((D // LANES, F2, LANES), F32), pltpu.SemaphoreType.DMA((3,))],
    )
    return pl.pallas_call(
        functools.partial(_expert_body, tile=tile, d_expert=F),
        out_shape=jax.ShapeDtypeStruct((n_rows * n_chunks, LANES), U32),
        grid_spec=grid_spec,
        compiler_params=_cparams(("arbitrary",), 56),
        name="experts",
    )(inv, texp, n_used, n2p, w1t, b1c, w2, b2)


def _combine_body(dest_ref, y_hbm, wts_ref, h_ref, mod_ref, *rest, tile, n_tiles, gate_idx, with_norm):
    if with_norm:
        nab_ref, o_ref, n1_ref, ybuf0, ybuf1, sem = rest
    else:
        o_ref, ybuf0, ybuf1, sem = rest
        nab_ref = n1_ref = None
    i = pl.program_id(0)
    bufs = (ybuf0, ybuf1)
    n_groups = tile // SUBLANES
    n_chunks = ybuf0.shape[1] // tile
    half = n_chunks * LANES
    chunk = min(half, 8 * LANES)

    def group_copies(t, g, s):
        out = []
        for j in range(SUBLANES):
            for k in range(TOP_K):
                p = dest_ref[(t * tile + g * SUBLANES + j) * TOP_K + k]
                src = y_hbm.at[pl.ds(pl.multiple_of(p * n_chunks, n_chunks), n_chunks), :]
                dst = bufs[s].at[k, pl.ds(pl.multiple_of((g * SUBLANES + j) * n_chunks, n_chunks), n_chunks), :]
                out.append(pltpu.make_async_copy(src, dst, sem.at[s]))
        return out

    @pl.when(i == 0)
    def _():
        def f(g, c):
            for n, cp in enumerate(group_copies(0, g, 0)):
                cp.start(priority=n % 2)
            return c
        lax.fori_loop(0, n_groups, f, 0)

    def step(s, prefetch):
        def wait_group(g, c):
            for cp in group_copies(i, g, s):
                cp.wait()
            return c

        lax.fori_loop(0, n_groups, wait_group, 0)

        def mix_pair(gp, c):
            for u in range(2):
                g = 2 * gp + u
                if prefetch:
                    for n, cp in enumerate(group_copies(i + 1, g, 1 - s)):
                        cp.start(priority=n % 2)
                rows = pl.ds(pl.multiple_of(g * SUBLANES, SUBLANES), SUBLANES)
                w_blk = wts_ref[rows, :]
                ws = [w_blk[:, k:k + 1] for k in range(TOP_K)]
                first = g * (SUBLANES * n_chunks)
                for c in range(n_chunks):
                    acc_lo = None
                    acc_hi = None
                    for k in range(TOP_K):
                        words = bufs[s][k, pl.ds(first + c, SUBLANES, stride=n_chunks), :]
                        lo, hi = _unpack_bf16_pairs(words)
                        acc_lo = ws[k] * lo if acc_lo is None else acc_lo + ws[k] * lo
                        acc_hi = ws[k] * hi if acc_hi is None else acc_hi + ws[k] * hi
                    for base, acc in ((c * LANES, acc_lo), (half + c * LANES, acc_hi)):
                        gate = mod_ref[0, 0, pl.ds(gate_idx, 1), base:base + LANES]
                        o_ref[rows, base:base + LANES] = h_ref[rows, base:base + LANES] + gate * acc
            if n1_ref is not None:
                rows2 = pl.ds(pl.multiple_of(gp * 2 * SUBLANES, 2 * SUBLANES), 2 * SUBLANES)
                bases = range(0, 2 * half, chunk)
                ssq = None
                for base in bases:
                    x = o_ref[rows2, base:base + chunk]
                    part = jnp.sum(x * x, axis=-1, keepdims=True)
                    ssq = part if ssq is None else ssq + part
                rinv = lax.rsqrt(ssq * (1.0 / (2 * half)) + EPS)
                for base in bases:
                    a = nab_ref[0, 0, 0:1, base:base + chunk]
                    b = nab_ref[0, 0, 1:2, base:base + chunk]
                    x = o_ref[rows2, base:base + chunk]
                    n1_ref[rows2, base:base + chunk] = (x * rinv * a + b).astype(n1_ref.dtype)
            return c

        lax.fori_loop(0, n_groups // 2, mix_pair, 0)

    last = n_tiles - 1
    for s in range(2):
        pl.when(jnp.logical_and(i < last, lax.rem(i, 2) == s))(functools.partial(step, s, True))
    pl.when(i == last)(functools.partial(step, last % 2, False))


def _combine(ysp, dest, wts, h, mods, layer, group_of_tile, tile, gate_idx, norm_ab=None):
    T, D = h.shape
    half = D // 2
    n_tiles = T // tile
    with_norm = norm_ab is not None
    row_spec = pl.BlockSpec((tile, D), lambda i, d: (i, 0))
    in_specs = [
        pl.BlockSpec(memory_space=pl.ANY),
        pl.BlockSpec((tile, LANES), lambda i, d: (i, 0)),
        row_spec,
        pl.BlockSpec((1, 1, N_MOD, D), lambda i, d: (layer, group_of_tile(i), 0, 0)),
    ]
    operands = [dest, ysp, wts, h, mods]
    out_shape = jax.ShapeDtypeStruct((T, D), F32)
    out_specs = row_spec
    if with_norm:
        in_specs.append(pl.BlockSpec((1, 1, 2, D), lambda i, d: (layer + 1, group_of_tile(i), 0, 0)))
        operands.append(norm_ab)
        out_shape = (out_shape, jax.ShapeDtypeStruct((T, D), BF16))
        out_specs = (row_spec, row_spec)
    grid_spec = pltpu.PrefetchScalarGridSpec(
        num_scalar_prefetch=1,
        grid=(n_tiles,),
        in_specs=in_specs,
        out_specs=out_specs,
        scratch_shapes=[pltpu.VMEM((TOP_K, tile * (half // LANES), LANES), U32),
                        pltpu.VMEM((TOP_K, tile * (half // LANES), LANES), U32),
                        pltpu.SemaphoreType.DMA((2,))],
    )
    return pl.pallas_call(
        functools.partial(_combine_body, tile=tile, n_tiles=n_tiles, gate_idx=gate_idx, with_norm=with_norm),
        out_shape=out_shape,
        grid_spec=grid_spec,
        compiler_params=_cparams(("arbitrary",), 56),
        name="combine",
    )(*operands)


@jax.jit
def _forward(x, c, ctx, c_ctx, ada_w, ada_b, norm1_w, norm2_w, w_in, ret_decay, ret_gn_w,
             sg_norm_w, sg_w, sg_b, pool_w, pool_b, pool_scale, w_branch, gate_down, gate_up,
             gate_b, w_out, router_w, router_b, w1, b1, w2, b2, final_norm_w):
    B, n_lat, D = x.shape
    n_ctx = ctx.shape[1]
    depth = ada_w.shape[0]
    W = N_RET_HEADS * LANES
    E, _, F2 = w1.shape[1:]
    F = F2 // 2
    T = B * (n_lat + n_ctx)
    assert B + 1 <= COND_ROWS and n_lat % GRID_W == 0 and E == N_EXPERTS and F <= MXU_DIM
    assert n_lat % RET_CHUNK == 0 and n_ctx % RET_CHUNK == 0 and (B * n_lat) % n_ctx == 0

    tg = _pick(math.gcd(n_lat, B * n_ctx), (512, 256, 128))
    group_of = lambda rows: (lambda i: jnp.minimum((i * rows) // n_lat, B))
    tm = _pick(T, (1088, 1024, 768, 512, 384, 256, 128))
    tc = _pick(tg, (COMBINE_TILE, 128))

    h = jnp.concatenate([x.reshape(B * n_lat, D), ctx.reshape(B * n_ctx, D)], axis=0)
    cond = jnp.zeros((COND_ROWS, D), F32).at[:B].set(c).at[B].set(c_ctx)
    mods = _adaln(cond, ada_w, ada_b).reshape(depth, COND_ROWS, N_MOD, D)

    t = jnp.arange(n_lat)
    n_freq = LANES // 4
    freqs = ROPE_BASE ** (-jnp.arange(n_freq, dtype=F32) / n_freq)
    ang_r = (t // GRID_W).astype(F32)[:, None] * freqs
    ang_c = (t % GRID_W).astype(F32)[:, None] * freqs
    cos_t = jnp.concatenate([jnp.cos(ang_r)] * 2 + [jnp.cos(ang_c)] * 2, axis=-1)
    sin_t = jnp.concatenate([-jnp.sin(ang_r), jnp.sin(ang_r), -jnp.sin(ang_c), jnp.sin(ang_c)], axis=-1)

    blk = jnp.arange(T // SG_CHUNK)
    lat_blk, ctx_blk = n_lat // SG_CHUNK, n_ctx // SG_CHUNK
    in_lat = blk < B * lat_blk
    local = jnp.where(in_lat, blk % lat_blk, (blk - B * lat_blk) % ctx_blk)
    seq_blk = jnp.where(in_lat, lat_blk, ctx_blk)
    first_flag = (local == 0).astype(I32)
    last_flag = (local == seq_blk - 1).astype(I32)

    log_gamma = jax.nn.log_sigmoid(ret_decay.astype(F32))
    vec = lambda a: a.reshape(depth, 1, a.shape[-1])
    norm1_v, norm2_v, gn_v, sgn_v, pb_v, ps_v = map(vec, (norm1_w, norm2_w, ret_gn_w, sg_norm_w, pool_b, pool_scale))
    sgb_full = jnp.repeat(jnp.swapaxes(sg_b, 1, 2), LANES, axis=2)
    sg_wb = sg_w.astype(BF16)
    pool_wb = pool_w.astype(BF16)
    gate_b4 = gate_b.reshape(depth, 3, 1, D)
    router_wp = jnp.zeros((depth, D, LANES), BF16).at[:, :, :E].set(router_w.astype(BF16))
    router_bp = jnp.zeros((depth, 1, LANES), F32).at[:, 0, :E].set(router_b)
    w1t = jnp.swapaxes(w1, 2, 3)
    b1c = (jnp.zeros((depth, E, 1, 2 * MXU_DIM), F32).at[:, :, 0, :F].set(b1[:, :, 0::2])
           .at[:, :, 0, MXU_DIM:MXU_DIM + F].set(b1[:, :, 1::2]))
    b2_4 = b2.reshape(depth, E, 1, D)

    n_pairs = T * TOP_K
    n_rows = -(-(n_pairs + E * (EXPERT_TILE - 1)) // EXPERT_TILE) * EXPERT_TILE
    tile_start = jnp.arange(n_rows // EXPERT_TILE, dtype=I32) * EXPERT_TILE
    expert_ids = jnp.arange(E, dtype=I32)

    norm_ab = jnp.stack([norm1_w[:, None, :] * (1.0 + mods[:, :, 1, :]), mods[:, :, 0, :]], axis=2)

    n1 = _norm_mod(h, norm1_v, mods, 0, group_of(tg), tg, 0, 1)
    for l in range(depth):
        z = _matmul(n1, w_in, l, tm, 512)
        r = _matmul(n1, gate_down, l, tm, gate_down.shape[2])
        ya_x, ya_c = _retention(z, log_gamma, cos_t, sin_t, gn_v, l, B, n_lat, n_ctx)
        yb, yc = _branches_bc(z, first_flag, last_flag, sgn_v, sg_wb, sgb_full, pool_wb, pb_v, ps_v, l)
        ya = jnp.concatenate([ya_x, ya_c], axis=0)
        m = _merge(r, ya, yb, yc, gate_up, gate_b4, w_branch, l, tm, 512)
        h = _matmul_residual(m, w_out, h, mods, l, n_lat, B + 1, tm, 512, 2)

        n2p, meta, wts, counts = _norm_router(h, norm2_v, mods, router_wp, router_bp, l, group_of(tg), tg, 3, 4)
        cnt = counts[0, :E].astype(I32)
        padded = ((cnt + EXPERT_TILE - 1) // EXPERT_TILE) * EXPERT_TILE
        po = jnp.concatenate([jnp.zeros((1,), I32), jnp.cumsum(padded, dtype=I32)])
        texp = jnp.minimum(jnp.sum(tile_start[:, None] >= po[None, 1:], axis=1, dtype=I32), E - 1)
        n_used = (po[E] // EXPERT_TILE).reshape(1)
        ids = meta[:, :TOP_K]
        start_of = jnp.sum(jnp.where(ids[:, :, None] == expert_ids, po[:E], 0), axis=-1, dtype=I32)
        dest = (start_of + meta[:, TOP_K:2 * TOP_K]).reshape(n_pairs)
        inv = _row_token(dest, po, cnt, n_rows)
        ysp = _experts(n2p, inv, texp, n_used, w1t, b1c, w2, b2_4, l, n_rows, EXPERT_TILE)
        if l + 1 < depth:
            h, n1 = _combine(ysp, dest, wts, h, mods, l, group_of(tc), tc, 5, norm_ab)
        else:
            h = _combine(ysp, dest, wts, h, mods, l, group_of(tc), tc, 5)

    out = _final_norm(h, final_norm_w, B * n_lat, tg)
    return out.reshape(B, n_lat, D)


def kernel(x, c, ctx, c_ctx, ada_w, ada_b, norm1_w, norm2_w, w_in, ret_decay, ret_gn_w, sg_norm_w, sg_w, sg_b, pool_w, pool_b, pool_scale, w_branch, gate_down, gate_up, gate_b, w_out, router_w, router_b, w1, b1, w2, b2, final_norm_w):
    return _forward(x, c, ctx, c_ctx, ada_w, ada_b, norm1_w, norm2_w, w_in, ret_decay, ret_gn_w,
                    sg_norm_w, sg_w, sg_b, pool_w, pool_b, pool_scale, w_branch, gate_down, gate_up,
                    gate_b, w_out, router_w, router_b, w1, b1, w2, b2, final_norm_w)
```
